```python
import math
import jax, jax.numpy as jnp
from jax import lax
import numpy as np

D_MODEL = 1024
BATCH = 8
SEQ = 4096
DEPTH = 2

GRID_W = 64
CTX_LEN = 256
F32 = jnp.float32
EPS = 1e-6
CONV_CH = 512
CONV_K = 31
MLA_HEADS = 8
QK_NOPE = 64
QK_ROPE = 32
V_DIM = 64
Q_RANK = 384
KV_RANK = 256
ROPE_BASE = 10000.0
Q_BLOCK = 128
MLA_SCALE = (QK_NOPE + QK_ROPE) ** -0.5
GDN_HEADS = 8
GDN_DK = 64
GDN_DV = 64
SHORT_K = 5
CHUNK = 64
N_GROUPS = 4
EXP_PER_GROUP = 8
N_EXPERTS = N_GROUPS * EXP_PER_GROUP
TOP_K = 2
D_EXPERT = 256
ROW_BLOCK = 128
ALPHA = (2 * DEPTH) ** 0.25
BETA_INIT = (8 * DEPTH) ** -0.25
IN_SIZES = (2 * CONV_CH, Q_RANK, KV_RANK, QK_ROPE, GDN_HEADS * GDN_DK, GDN_HEADS * GDN_DK, GDN_HEADS * GDN_DV, GDN_HEADS * GDN_DV, 2 * GDN_HEADS, 2 * GDN_HEADS, 3 * D_MODEL)
P_IN = sum(IN_SIZES)

kernel_name = "hybrid_conv_mla_gdn_hmoe_diffusion_block"


def _ln(x, g, b):
    xf = x.astype(F32)
    mu = jnp.mean(xf, -1, keepdims=True)
    var = jnp.mean(jnp.square(xf - mu), -1, keepdims=True)
    return ((xf - mu) * lax.rsqrt(var + EPS)).astype(x.dtype) * g + b


def _rms(x, g):
    xf = x.astype(F32)
    return (xf * lax.rsqrt(jnp.mean(jnp.square(xf), -1, keepdims=True) + EPS)).astype(x.dtype) * g


def _l2n(x):
    xf = x.astype(F32)
    return (xf * lax.rsqrt(jnp.sum(jnp.square(xf), -1, keepdims=True) + EPS)).astype(x.dtype)


def _split(y, sizes):
    return jnp.split(y, np.cumsum(sizes)[:-1].tolist(), axis=-1)


def _dwconv(u, w):
    k = w.shape[0]
    return lax.conv_general_dilated(u, w[:, None, :], (1,), [(k // 2, k // 2)],
                                    dimension_numbers=("NWC", "WIO", "NWC"),
                                    feature_group_count=u.shape[-1])


def _axial_rope(rows):
    nf = QK_ROPE // 4
    inv = ROPE_BASE ** (-jnp.arange(nf, dtype=F32) / nf)
    r = jnp.repeat(jnp.arange(rows, dtype=F32), GRID_W)
    col = jnp.tile(jnp.arange(GRID_W, dtype=F32), rows)
    ang = jnp.stack([r[:, None] * inv, col[:, None] * inv], axis=1)
    return jnp.cos(ang), jnp.sin(ang)


def _rope(x, cos, sin):
    xs = x.reshape(x.shape[:-1] + (2, 2, QK_ROPE // 4))
    x1, x2 = xs[..., 0, :], xs[..., 1, :]
    cos = cos.astype(x.dtype)
    sin = sin.astype(x.dtype)
    out = jnp.stack([x1 * cos - x2 * sin, x2 * cos + x1 * sin], axis=-2)
    return out.reshape(x.shape)


def _conv_branch(u, p):
    a, gt = jnp.split(u, 2, axis=-1)
    u = a * jax.nn.sigmoid(gt)
    u = _dwconv(u, p["conv_a_w"]) + p["conv_a_b"]
    u = jax.nn.silu(_ln(u, p["ln_a_g"], p["ln_a_b"]))
    return u @ p["w_a_out"]


def _mla_q(qd, p, cos, sin):
    b, n = qd.shape[:2]
    cq = _rms(qd, p["g_q"])
    q_nope = (cq @ p["w_uq"]).reshape(b, n, MLA_HEADS, QK_NOPE)
    q_rope = (cq @ p["w_qr"]).reshape(b, n, MLA_HEADS, QK_ROPE)
    if cos is not None:
        q_rope = _rope(q_rope, cos[:, None], sin[:, None])
    return jnp.concatenate([q_nope, q_rope], -1) * MLA_SCALE


def _mla_kv(kvd, kr, p, cos, sin):
    b, n = kvd.shape[:2]
    ckv = _rms(kvd, p["g_kv"])
    k_nope = (ckv @ p["w_uk"]).reshape(b, n, MLA_HEADS, QK_NOPE)
    v = (ckv @ p["w_uv"]).reshape(b, n, MLA_HEADS, V_DIM)
    if cos is not None:
        kr = _rope(kr, cos, sin)
    k_rope = jnp.broadcast_to(kr[:, :, None, :], (b, n, MLA_HEADS, QK_ROPE))
    return jnp.concatenate([k_nope, k_rope], -1), v


def _attend(q, k, v):
    s = jnp.einsum("bqhd,bkhd->bhqk", q, k).astype(F32)
    pr = jax.nn.softmax(s, axis=-1).astype(v.dtype)
    return jnp.einsum("bhqk,bkhd->bqhd", pr, v)


def _attend_blocks(q, k, v):
    b, n, h, dq = q.shape
    qb = q.reshape(b, n // Q_BLOCK, Q_BLOCK, h, dq).transpose(1, 0, 2, 3, 4)
    o = lax.map(lambda qq: _attend(qq, k, v), qb)
    return o.transpose(1, 0, 2, 3, 4).reshape(b, n, h, v.shape[-1])


def _gdn_inputs(gq, gk, gv, braw, araw, p, need_q):
    qk = GDN_HEADS * GDN_DK
    b, n = gk.shape[:2]
    if need_q:
        u = jax.nn.silu(_dwconv(jnp.concatenate([gq, gk, gv], -1), p["conv_c_w"]))
        q, u = u[..., :qk], u[..., qk:]
        q = _l2n(q.reshape(b, n, GDN_HEADS, GDN_DK)) * (GDN_DK ** -0.5)
    else:
        u = jax.nn.silu(_dwconv(jnp.concatenate([gk, gv], -1), p["conv_c_w"][:, qk:]))
        q = None
    k = _l2n(u[..., :qk].reshape(b, n, GDN_HEADS, GDN_DK))
    v = u[..., qk:].reshape(b, n, GDN_HEADS, GDN_DV)
    beta = jax.nn.sigmoid(braw.astype(F32)).reshape(b, n, 2, GDN_HEADS)
    g = -jnp.exp(p["a_log"].astype(F32)) * jax.nn.softplus(araw.astype(F32).reshape(b, n, 2, GDN_HEADS) + p["dt_bias"].astype(F32))
    return q, k, v, beta, g


def _direction(inp, d):
    q, k, v, beta, g = inp
    beta, g = beta[:, :, d], g[:, :, d]
    if d == 0:
        return q, k, v, beta, g
    fl = lambda t: None if t is None else jnp.flip(t, 1)
    return fl(q), fl(k), fl(v), fl(beta), fl(g)


def _gdn_scan(q, k, v, beta, g, s0):
    b, n, h, dk = k.shape
    dv = v.shape[-1]
    nc = n // CHUNK
    out_dtype = v.dtype

    def blk(t):
        return t.astype(F32).reshape(b, nc, CHUNK, h, -1).transpose(1, 0, 3, 2, 4)

    k_, v_ = blk(k), blk(v)
    beta_ = blk(beta[..., None])[..., 0]
    gam = jnp.cumsum(blk(g[..., None])[..., 0], axis=-1)
    incl = jnp.tril(jnp.ones((CHUNK, CHUNK), dtype=bool))
    strict = jnp.tril(jnp.ones((CHUNK, CHUNK), dtype=bool), -1)
    diff = gam[..., :, None] - gam[..., None, :]
    decay = jnp.where(incl, jnp.exp(jnp.where(incl, diff, 0.0)), 0.0)
    kk = jnp.einsum("nbhid,nbhjd->nbhij", k_, k_)
    a_mat = jnp.where(strict, beta_[..., :, None] * decay * kk, 0.0) + jnp.eye(CHUNK, dtype=F32)
    rhs = jnp.concatenate([(beta_ * jnp.exp(gam))[..., None] * k_, beta_[..., None] * v_], -1)
    sol = lax.linalg.triangular_solve(a_mat, rhs, left_side=True, lower=True, unit_diagonal=True)
    w, u0 = sol[..., :dk], sol[..., dk:]
    kd = k_ * jnp.exp(gam[..., -1:] - gam)[..., None]
    gc = jnp.exp(gam[..., -1])[..., None, None]

    def advance(s, w_c, u0_c, kd_c, gc_c):
        u = u0_c - jnp.einsum("bhcd,bhde->bhce", w_c, s)
        return gc_c * s + jnp.einsum("bhcd,bhce->bhde", kd_c, u), u

    if q is None:
        def step_state(s, xs):
            s_new, _ = advance(s, *xs)
            return s_new, None
        s_fin, _ = lax.scan(step_state, s0, (w, u0, kd, gc))
        return None, s_fin

    q_ = blk(q)
    qg = q_ * jnp.exp(gam)[..., None]
    pm = decay * jnp.einsum("nbhid,nbhjd->nbhij", q_, k_)

    def step(s, xs):
        w_c, u0_c, kd_c, gc_c, qg_c, pm_c = xs
        s_new, u = advance(s, w_c, u0_c, kd_c, gc_c)
        o = jnp.einsum("bhcd,bhde->bhce", qg_c, s) + jnp.einsum("bhij,bhje->bhie", pm_c, u)
        return s_new, o

    s_fin, o = lax.scan(step, s0, (w, u0, kd, gc, qg, pm))
    o = o.transpose(1, 0, 3, 2, 4).reshape(b, n, h, dv).astype(out_dtype)
    return o, s_fin


def _gdn_bidir(gin_c, gin_l):
    b = gin_l[1].shape[0]
    s0 = jnp.zeros((b, GDN_HEADS, GDN_DK, GDN_DV), F32)
    outs_c, outs_l = [], []
    for d in range(2):
        oc, s_c = _gdn_scan(*_direction(gin_c, d), s0)
        ol, _ = _gdn_scan(*_direction(gin_l, d), s_c)
        if d == 1:
            ol = jnp.flip(ol, 1)
            oc = None if oc is None else jnp.flip(oc, 1)
        outs_c.append(oc)
        outs_l.append(ol)
    o_c = None if outs_c[0] is None else outs_c[0] + outs_c[1]
    return o_c, outs_l[0] + outs_l[1]


def _gdn_out(o, gate, p):
    b, n = o.shape[:2]
    o = _rms(o, p["g_o"]) * jax.nn.silu(gate.reshape(b, n, GDN_HEADS, GDN_DV))
    return o.reshape(b, n, -1) @ p["w_c_out"]


def _merge(gate_cols, ya, yb, yc, p):
    ga, gb, gc = jnp.split(jax.nn.sigmoid(gate_cols), 3, axis=-1)
    return (ga * ya + gb * yb + gc * yc) @ p["w_out"]


def _mixer_layer(hc, hl, p, cos, sin, ctx_out):
    pc = _split(hc @ p["w_in"] + p["b_in"], IN_SIZES)
    pl = _split(hl @ p["w_in"] + p["b_in"], IN_SIZES)
    b, n = hl.shape[:2]
    ya_l = _conv_branch(pl[0], p)
    k_c, v_c = _mla_kv(pc[2], pc[3], p, None, None)
    k_l, v_l = _mla_kv(pl[2], pl[3], p, cos, sin)
    q_l = _mla_q(pl[1], p, cos, sin)
    yb_l = _attend_blocks(q_l, jnp.concatenate([k_c, k_l], 1), jnp.concatenate([v_c, v_l], 1))
    yb_l = yb_l.reshape(b, n, -1) @ p["w_b_out"]
    gin_c = _gdn_inputs(pc[4], pc[5], pc[6], pc[8], pc[9], p, ctx_out)
    gin_l = _gdn_inputs(pl[4], pl[5], pl[6], pl[8], pl[9], p, True)
    o_c, o_l = _gdn_bidir(gin_c, gin_l)
    yc_l = _gdn_out(o_l, pl[7], p)
    out_l = _merge(pl[10], ya_l, yb_l, yc_l, p)
    if not ctx_out:
        return None, out_l
    bc, nc = hc.shape[:2]
    ya_c = _conv_branch(pc[0], p)
    yb_c = _attend(_mla_q(pc[1], p, None, None), k_c, v_c).reshape(bc, nc, -1) @ p["w_b_out"]
    yc_c = _gdn_out(o_c, pc[7], p)
    out_c = _merge(pc[10], ya_c, yb_c, yc_c, p)
    return out_c, out_l


def _moe(h, p):
    shp = h.shape
    t = h.reshape(-1, shp[-1])
    n_tok = t.shape[0]
    tf = t.astype(F32)
    grp_prob = jax.nn.softmax(tf @ p["w_rg"].astype(F32) + p["b_rg"].astype(F32), axis=-1)
    gp, gi = lax.top_k(grp_prob, 1)
    e_logits = (tf @ p["w_re"].astype(F32) + p["b_re"].astype(F32)).reshape(n_tok, N_GROUPS, EXP_PER_GROUP)
    in_grp = jnp.take_along_axis(e_logits, gi[:, :, None], axis=1)[:, 0]
    ev, ei = lax.top_k(in_grp, TOP_K)
    gate = (gp * jax.nn.softmax(ev, axis=-1)).reshape(-1)
    eid = (gi * EXP_PER_GROUP + ei).reshape(-1)
    tok = jnp.repeat(jnp.arange(n_tok, dtype=jnp.int32), TOP_K)
    order = jnp.argsort(eid)
    es, ts, gs = eid[order], tok[order], gate[order]
    n_as = es.shape[0]
    sizes = jnp.bincount(es, length=N_EXPERTS)
    psizes = (sizes + ROW_BLOCK - 1) // ROW_BLOCK * ROW_BLOCK
    starts = jnp.cumsum(sizes) - sizes
    pends = jnp.cumsum(psizes)
    pstarts = pends - psizes
    dest = pstarts[es] + jnp.arange(n_as, dtype=jnp.int32) - starts[es]
    n_blk = (n_as + ROW_BLOCK - 1) // ROW_BLOCK + N_EXPERTS
    n_pad = n_blk * ROW_BLOCK
    tok_buf = jnp.zeros((n_pad,), jnp.int32).at[dest].set(ts)
    gate_buf = jnp.zeros((n_pad,), F32).at[dest].set(gs)
    blk_exp = jnp.minimum(jnp.searchsorted(pends, jnp.arange(n_blk, dtype=pends.dtype) * ROW_BLOCK, side="right"), N_EXPERTS - 1)
    xb = t[tok_buf].reshape(n_blk, ROW_BLOCK, -1)

    def expert_block(args):
        xx, e = args
        return (jax.nn.silu(xx @ p["w1"][e]) * (xx @ p["w3"][e])) @ p["w2"][e]

    yb = lax.map(expert_block, (xb, blk_exp)).reshape(n_pad, -1)
    y = jnp.zeros_like(t).at[tok_buf].add(yb * gate_buf[:, None].astype(t.dtype))
    return y.reshape(shp)


def setup_inputs(seed: int = 0) -> dict:
    key = jax.random.key(seed)
    ks = iter(jax.random.split(key, 48))
    L, D = DEPTH, D_MODEL

    def nrm(shape, scale):
        return jax.random.normal(next(ks), shape, F32) * scale

    def gain(shape):
        return 1.0 + nrm(shape, 0.02)

    out = {}
    out["x"] = nrm((BATCH, SEQ, D), 1.0)
    out["c"] = nrm((BATCH, D), 1.0)
    out["ctx"] = nrm((BATCH, CTX_LEN, D), 1.0)
    out["c_ctx"] = nrm((D,), 1.0)
    out["w_mod"] = nrm((L, D, 6 * D), 0.5 * D ** -0.5)
    out["b_mod"] = nrm((L, 6 * D), 0.02)
    out["w_in"] = nrm((L, D, P_IN), D ** -0.5)
    out["b_in"] = nrm((L, P_IN), 0.02)
    out["conv_a_w"] = nrm((L, CONV_K, CONV_CH), CONV_K ** -0.5)
    out["conv_a_b"] = nrm((L, CONV_CH), 0.02)
    out["ln_a_g"] = gain((L, CONV_CH))
    out["ln_a_b"] = nrm((L, CONV_CH), 0.02)
    out["w_a_out"] = nrm((L, CONV_CH, D), CONV_CH ** -0.5)
    out["g_q"] = gain((L, Q_RANK))
    out["g_kv"] = gain((L, KV_RANK))
    out["w_uq"] = nrm((L, Q_RANK, MLA_HEADS * QK_NOPE), Q_RANK ** -0.5)
    out["w_qr"] = nrm((L, Q_RANK, MLA_HEADS * QK_ROPE), Q_RANK ** -0.5)
    out["w_uk"] = nrm((L, KV_RANK, MLA_HEADS * QK_NOPE), KV_RANK ** -0.5)
    out["w_uv"] = nrm((L, KV_RANK, MLA_HEADS * V_DIM), KV_RANK ** -0.5)
    out["w_b_out"] = nrm((L, MLA_HEADS * V_DIM, D), (MLA_HEADS * V_DIM) ** -0.5)
    out["conv_c_w"] = nrm((L, SHORT_K, GDN_HEADS * (2 * GDN_DK + GDN_DV)), SHORT_K ** -0.5)
    out["a_log"] = jnp.log(jax.random.uniform(next(ks), (L, 2, GDN_HEADS), F32, 1.0, 16.0))
    dt = jnp.exp(jax.random.uniform(next(ks), (L, 2, GDN_HEADS), F32, math.log(1e-3), math.log(0.1)))
    out["dt_bias"] = dt + jnp.log(-jnp.expm1(-dt))
    out["g_o"] = gain((L, GDN_DV))
    out["w_c_out"] = nrm((L, GDN_HEADS * GDN_DV, D), (GDN_HEADS * GDN_DV) ** -0.5)
    out["w_out"] = nrm((L, D, D), BETA_INIT * D ** -0.5)
    out["ln1_g"] = gain((L, D))
    out["ln1_b"] = nrm((L, D), 0.02)
    out["w_rg"] = nrm((L, D, N_GROUPS), D ** -0.5)
    out["b_rg"] = nrm((L, N_GROUPS), 0.01)
    out["w_re"] = nrm((L, D, N_EXPERTS), D ** -0.5)
    out["b_re"] = nrm((L, N_EXPERTS), 0.01)
    out["w1"] = nrm((L, N_EXPERTS, D, D_EXPERT), D ** -0.5)
    out["w3"] = nrm((L, N_EXPERTS, D, D_EXPERT), D ** -0.5)
    out["w2"] = nrm((L, N_EXPERTS, D_EXPERT, D), BETA_INIT * D_EXPERT ** -0.5)
    out["ln2_g"] = gain((L, D))
    out["ln2_b"] = nrm((L, D), 0.02)
    return out


def reference(x, c, ctx, c_ctx, w_mod, b_mod, w_in, b_in, conv_a_w, conv_a_b, ln_a_g, ln_a_b, w_a_out,
              g_q, g_kv, w_uq, w_qr, w_uk, w_uv, w_b_out, conv_c_w, a_log, dt_bias, g_o, w_c_out,
              w_out, ln1_g, ln1_b, w_rg, b_rg, w_re, b_re, w1, w3, w2, ln2_g, ln2_b):
    rows = x.shape[1] // GRID_W
    cos, sin = _axial_rope(rows)
    xl, xc = x, ctx
    n_ctx = ctx.shape[1]
    for l in range(DEPTH):
        last = l == DEPTH - 1
        p = dict(w_in=w_in[l], b_in=b_in[l], conv_a_w=conv_a_w[l], conv_a_b=conv_a_b[l], ln_a_g=ln_a_g[l],
                 ln_a_b=ln_a_b[l], w_a_out=w_a_out[l], g_q=g_q[l], g_kv=g_kv[l], w_uq=w_uq[l], w_qr=w_qr[l],
                 w_uk=w_uk[l], w_uv=w_uv[l], w_b_out=w_b_out[l], conv_c_w=conv_c_w[l], a_log=a_log[l],
                 dt_bias=dt_bias[l], g_o=g_o[l], w_c_out=w_c_out[l], w_out=w_out[l], w_rg=w_rg[l], b_rg=b_rg[l],
                 w_re=w_re[l], b_re=b_re[l], w1=w1[l], w3=w3[l], w2=w2[l])
        mod_l = (jax.nn.silu(c) @ w_mod[l] + b_mod[l])[:, None, :]
        mod_c = (jax.nn.silu(c_ctx) @ w_mod[l] + b_mod[l])[None, None, :]
        sh1l, sc1l, g1l, sh2l, sc2l, g2l = jnp.split(mod_l, 6, axis=-1)
        sh1c, sc1c, g1c, sh2c, sc2c, g2c = jnp.split(mod_c, 6, axis=-1)
        hl = xl * (1 + sc1l) + sh1l
        hc = xc * (1 + sc1c) + sh1c
        yc, yl = _mixer_layer(hc, hl, p, cos, sin, not last)
        xl = _ln(ALPHA * xl + g1l * yl, ln1_g[l], ln1_b[l])
        h2l = xl * (1 + sc2l) + sh2l
        if last:
            y2l = _moe(h2l, p)
        else:
            xc = _ln(ALPHA * xc + g1c * yc, ln1_g[l], ln1_b[l])
            h2c = xc * (1 + sc2c) + sh2c
            y2 = _moe(jnp.concatenate([h2c, h2l], axis=1), p)
            y2c, y2l = y2[:, :n_ctx], y2[:, n_ctx:]
            xc = _ln(ALPHA * xc + g2c * y2c, ln2_g[l], ln2_b[l])
        xl = _ln(ALPHA * xl + g2l * y2l, ln2_g[l], ln2_b[l])
    return xl
```

```python
import functools

import jax
import jax.numpy as jnp
import numpy as np
from jax import lax
from jax.experimental import pallas as pl
from jax.experimental.pallas import tpu as pltpu

F32 = jnp.float32
BF16 = jnp.bfloat16

D_MODEL = 1024
GRID_W = 64
EPS = 1e-6
CONV_CH = 512
CONV_K = 31
MLA_HEADS = 8
QK_NOPE = 64
QK_ROPE = 32
V_DIM = 64
Q_RANK = 384
KV_RANK = 256
ROPE_BASE = 10000.0
MLA_SCALE = (QK_NOPE + QK_ROPE) ** -0.5
GDN_HEADS = 8
GDN_DK = 64
GDN_DV = 64
SHORT_K = 5
CHUNK = 64
N_GROUPS = 4
EXP_PER_GROUP = 8
N_EXPERTS = N_GROUPS * EXP_PER_GROUP
D_EXPERT = 256

LANE = 128
TM = 256
HEAD_PAD = LANE
MOE_ROWS = 256
VMEM_LIMIT = 56 * 1024 * 1024

OFF_GLU = 0
OFF_QD = OFF_GLU + 2 * CONV_CH
OFF_KVD = OFF_QD + Q_RANK
OFF_KR = OFF_KVD + KV_RANK
OFF_GQKV = OFF_KR + 2 * LANE
OFF_GG = OFF_GQKV + 3 * GDN_HEADS * GDN_DK
OFF_BA = OFF_GG + GDN_HEADS * GDN_DV
OFF_MG = OFF_BA + LANE
P_ALL = OFF_MG + 3 * D_MODEL


def _cparams(n_grid=1):
    return pltpu.CompilerParams(dimension_semantics=("arbitrary",) * n_grid, vmem_limit_bytes=VMEM_LIMIT)


def _const_spec(shape):
    nd = len(shape)
    return pl.BlockSpec(shape, lambda *_: (0,) * nd, pipeline_mode=pl.Buffered(1))


def _bdot(a, b):
    return jnp.dot(a.astype(BF16), b.astype(BF16), preferred_element_type=F32)


def _hdot(a, b):
    return jnp.dot(a, b, precision=lax.Precision.HIGHEST, preferred_element_type=F32)


def _dot_nt(a, b):
    return lax.dot_general(a.astype(BF16), b.astype(BF16), (((1,), (1,)), ((), ())), preferred_element_type=F32)


def _dot_tn(a, b):
    return lax.dot_general(a.astype(BF16), b.astype(BF16), (((0,), (0,)), ((), ())), preferred_element_type=F32)


def _silu(x):
    return x * jax.nn.sigmoid(x)


def _group_sumsq(x, ones_blk):
    x2 = x * x
    hi = x2.astype(BF16)
    lo = (x2 - hi.astype(F32)).astype(BF16)
    return jnp.dot(hi, ones_blk, preferred_element_type=F32) + jnp.dot(lo, ones_blk, preferred_element_type=F32)


def _layernorm(x, g, b):
    mu = jnp.mean(x, -1, keepdims=True)
    xc = x - mu
    var = jnp.mean(xc * xc, -1, keepdims=True)
    return xc * lax.rsqrt(var + EPS) * g + b


def _mod_kernel(c_ref, w_ref, b_ref, o_ref):
    o_ref[...] = _bdot(_silu(c_ref[...]), w_ref[...]) + b_ref[...]


def _mod_call(cvec, w, b):
    rows, d = cvec.shape
    n = w.shape[1]
    tn = 1536
    return pl.pallas_call(
        _mod_kernel,
        grid=(n // tn,),
        in_specs=[pl.BlockSpec((rows, d), lambda j: (0, 0)),
                  pl.BlockSpec((d, tn), lambda j: (0, j)),
                  pl.BlockSpec((1, tn), lambda j: (0, j))],
        out_specs=pl.BlockSpec((rows, tn), lambda j: (0, j)),
        out_shape=jax.ShapeDtypeStruct((rows, n), F32),
        compiler_params=_cparams(),
        name="mod_vectors",
    )(cvec, w, b)


def _inproj_kernel(x_ref, mod_ref, w_ref, b_ref, gpar_ref,
                   u_ref, qd_ref, kvd_ref, kr_ref, gqkv_ref, gg_ref, ba_ref, mg_ref):
    x = x_ref[...]
    sh = mod_ref[0, 0:1, :]
    sc = mod_ref[0, 1:2, :]
    h = (x * (1.0 + sc) + sh).astype(BF16)

    def proj(off, n):
        return jnp.dot(h, w_ref[:, off:off + n], preferred_element_type=F32) + b_ref[:, off:off + n]

    y = proj(OFF_GLU, 2 * CONV_CH)
    u_ref[...] = y[:, :CONV_CH] * jax.nn.sigmoid(y[:, CONV_CH:])
    qd_ref[...] = proj(OFF_QD, Q_RANK)
    kvd_ref[...] = proj(OFF_KVD, KV_RANK)
    kr_ref[...] = proj(OFF_KR, 2 * LANE)
    gqkv_ref[...] = proj(OFF_GQKV, 3 * GDN_HEADS * GDN_DK)
    gg_ref[...] = proj(OFF_GG, GDN_HEADS * GDN_DV)
    y = proj(OFF_BA, LANE)
    lane = lax.broadcasted_iota(jnp.int32, y.shape, 1)
    z = y + gpar_ref[1:2, :]
    sp = jnp.maximum(z, 0.0) + jnp.log1p(jnp.exp(-jnp.abs(z)))
    ba_ref[...] = jnp.where(lane < 2 * GDN_HEADS, jax.nn.sigmoid(y), -jnp.exp(gpar_ref[0:1, :]) * sp)
    mg_ref[...] = jax.nn.sigmoid(proj(OFF_MG, 3 * D_MODEL)).astype(BF16)


def _inproj_call(xcat, modrows, w_all, b_all, gpar, tpb, n_ctx_tiles):
    m, d = xcat.shape
    nt = m // TM
    n_batch = modrows.shape[0] - 1

    def mod_map(t):
        return (jnp.where(t % tpb < n_ctx_tiles, n_batch, t // tpb), 0, 0)

    widths = (CONV_CH, Q_RANK, KV_RANK, 2 * LANE, 3 * GDN_HEADS * GDN_DK, GDN_HEADS * GDN_DV, LANE, 3 * D_MODEL)
    dtypes = (F32, F32, F32, F32, F32, F32, F32, BF16)
    return pl.pallas_call(
        _inproj_kernel,
        grid=(nt,),
        in_specs=[pl.BlockSpec((TM, d), lambda t: (t, 0)),
                  pl.BlockSpec((1, 6, d), mod_map),
                  _const_spec(w_all.shape), _const_spec(b_all.shape), _const_spec(gpar.shape)],
        out_specs=[pl.BlockSpec((TM, w), lambda t: (t, 0)) for w in widths],
        out_shape=[jax.ShapeDtypeStruct((m, w), dt) for w, dt in zip(widths, dtypes)],
        compiler_params=_cparams(),
        name="in_proj",
    )(xcat, modrows, w_all, b_all, gpar)


CONV_HALO = 16


def _conv_a_kernel(top_ref, cur_ref, bot_ref, w_ref, par_ref, o_ref, win_ref, *, tpb, n_ctx_tiles):
    t = pl.program_id(0)
    r = t % tpb
    top_ok = jnp.logical_and(r != 0, r != n_ctx_tiles)
    bot_ok = jnp.logical_and(r != n_ctx_tiles - 1, r != tpb - 1)
    win_ref[0:CONV_HALO, :] = jnp.where(top_ok, top_ref[...], 0.0)
    win_ref[CONV_HALO:CONV_HALO + TM, :] = cur_ref[...]
    win_ref[CONV_HALO + TM:, :] = jnp.where(bot_ok, bot_ref[...], 0.0)
    acc = jnp.zeros((TM, CONV_CH), F32)
    base = CONV_HALO - CONV_K // 2
    for k in range(CONV_K):
        acc = acc + win_ref[pl.ds(base + k, TM), :] * w_ref[k:k + 1, :]
    y = _layernorm(acc + par_ref[0:1, :], par_ref[1:2, :], par_ref[2:3, :])
    o_ref[...] = _silu(y).astype(BF16)


def _conv_a_call(u, w, par, tpb, n_ctx_tiles):
    m, c = u.shape
    nt = m // TM
    hb = TM // CONV_HALO
    last = m // CONV_HALO - 1
    wp = jnp.zeros((32, c), F32).at[:CONV_K].set(w)
    return pl.pallas_call(
        functools.partial(_conv_a_kernel, tpb=tpb, n_ctx_tiles=n_ctx_tiles),
        grid=(nt,),
        in_specs=[pl.BlockSpec((CONV_HALO, c), lambda t: (jnp.maximum(t * hb - 1, 0), 0)),
                  pl.BlockSpec((TM, c), lambda t: (t, 0)),
                  pl.BlockSpec((CONV_HALO, c), lambda t: (jnp.minimum((t + 1) * hb, last), 0)),
                  _const_spec(wp.shape), _const_spec(par.shape)],
        out_specs=pl.BlockSpec((TM, c), lambda t: (t, 0)),
        out_shape=jax.ShapeDtypeStruct((m, c), BF16),
        scratch_shapes=[pltpu.VMEM((TM + 2 * CONV_HALO, c), F32)],
        compiler_params=_cparams(),
        name="conv_branch",
    )(u, u, u, wp, par)


def _mla_proj_kernel(qd_ref, kvd_ref, kr_ref, tqc_ref, tqs_ref, tkc_ref, tks_ref, gq_ref, gkv_ref,
                     wq_ref, wqr_ref, wk_ref, wv_ref, vone_ref, q_ref, k_ref, v_ref):
    qd = qd_ref[...]
    cq = qd * lax.rsqrt(jnp.mean(qd * qd, -1, keepdims=True) + EPS) * gq_ref[...]
    qf = _bdot(cq, wq_ref[...])
    qr = _bdot(cq, wqr_ref[...])
    kvd = kvd_ref[...]
    ckv = kvd * lax.rsqrt(jnp.mean(kvd * kvd, -1, keepdims=True) + EPS) * gkv_ref[...]
    kn = _bdot(ckv, wk_ref[...])
    krf = kr_ref[:, :LANE] * tkc_ref[...] + kr_ref[:, LANE:] * tks_ref[...]
    tqc = tqc_ref[...]
    tqs = tqs_ref[...]
    for h in range(MLA_HEADS):
        sl = slice(h * HEAD_PAD, (h + 1) * HEAD_PAD)
        q_ref[:, sl] = (qf[:, sl] * tqc + qr[:, sl] * tqs).astype(BF16)
        k_ref[:, sl] = (kn[:, sl] + krf).astype(BF16)
    v_ref[...] = (_bdot(ckv, wv_ref[...]) + vone_ref[...]).astype(BF16)


def _mla_proj_call(qd, kvd, kr, tabs, gq, gkv, wq, wqr, wk, wv, vone, tpb):
    m = qd.shape[0]
    nt = m // TM
    hp = MLA_HEADS * HEAD_PAD
    tab_spec = pl.BlockSpec((TM, LANE), lambda t: (t % tpb, 0))
    row = lambda w: pl.BlockSpec((TM, w), lambda t: (t, 0))
    return pl.pallas_call(
        _mla_proj_kernel,
        grid=(nt,),
        in_specs=[row(Q_RANK), row(KV_RANK), row(2 * LANE), tab_spec, tab_spec, tab_spec, tab_spec,
                  _const_spec(gq.shape), _const_spec(gkv.shape), _const_spec(wq.shape), _const_spec(wqr.shape),
                  _const_spec(wk.shape), _const_spec(wv.shape), _const_spec(vone.shape)],
        out_specs=[row(hp), row(hp), row(hp)],
        out_shape=[jax.ShapeDtypeStruct((m, hp), BF16)] * 3,
        compiler_params=_cparams(),
        name="mla_proj",
    )(qd, kvd, kr, *tabs, gq, gkv, wq, wqr, wk, wv, vone)


def _attn_kernel(q_ref, k_ref, v_ref, o_ref):
    for h in range(MLA_HEADS):
        sl = slice(h * HEAD_PAD, (h + 1) * HEAD_PAD)
        s = lax.dot_general(q_ref[:, sl], k_ref[:, sl], (((1,), (1,)), ((), ())), preferred_element_type=F32)
        m = jnp.max(s, axis=-1, keepdims=True)
        p = jnp.exp(s - m).astype(BF16)
        o = jnp.dot(p, v_ref[:, sl], preferred_element_type=F32)
        denom = o[:, V_DIM:V_DIM + 1]
        o_ref[:, sl] = (o * (1.0 / denom)).astype(BF16)


def _attn_call(q, k, v, n_batch, s_len, q_start, q_len, k_len, tq):
    hp = MLA_HEADS * HEAD_PAD
    nq = q_len // tq
    q0 = q_start // tq
    spb = s_len // tq
    kb = s_len // k_len
    return pl.pallas_call(
        _attn_kernel,
        grid=(n_batch, nq),
        in_specs=[pl.BlockSpec((tq, hp), lambda b, i: (b * spb + q0 + i, 0)),
                  pl.BlockSpec((k_len, hp), lambda b, i: (b * kb, 0), pipeline_mode=pl.Buffered(1)),
                  pl.BlockSpec((k_len, hp), lambda b, i: (b * kb, 0), pipeline_mode=pl.Buffered(1))],
        out_specs=pl.BlockSpec((tq, hp), lambda b, i: (b * nq + i, 0)),
        out_shape=jax.ShapeDtypeStruct((n_batch * q_len, hp), BF16),
        compiler_params=_cparams(2),
        name="attention",
    )(q, k, v)


GDN_HALO = 8


def _gdn_prep_kernel(top_ref, cur_ref, bot_ref, w_ref, ones_ref, o_ref, win_ref, *, tpb, n_ctx_tiles):
    t = pl.program_id(0)
    r = t % tpb
    top_ok = jnp.logical_and(r != 0, r != n_ctx_tiles)
    bot_ok = jnp.logical_and(r != n_ctx_tiles - 1, r != tpb - 1)
    win_ref[0:GDN_HALO, :] = jnp.where(top_ok, top_ref[...], 0.0)
    win_ref[GDN_HALO:GDN_HALO + TM, :] = cur_ref[...]
    win_ref[GDN_HALO + TM:, :] = jnp.where(bot_ok, bot_ref[...], 0.0)
    c = 3 * GDN_HEADS * GDN_DK
    acc = jnp.zeros((TM, c), F32)
    base = GDN_HALO - SHORT_K // 2
    for k in range(SHORT_K):
        acc = acc + win_ref[pl.ds(base + k, TM), :] * w_ref[k:k + 1, :]
    u = _silu(acc)
    qk = GDN_HEADS * GDN_DK
    q = u[:, :qk]
    kk = u[:, qk:2 * qk]
    ones_blk = ones_ref[...]
    q = q * lax.rsqrt(_group_sumsq(q, ones_blk) + EPS) * (GDN_DK ** -0.5)
    kk = kk * lax.rsqrt(_group_sumsq(kk, ones_blk) + EPS)
    o_ref[:, :qk] = q.astype(BF16)
    o_ref[:, qk:2 * qk] = kk.astype(BF16)
    o_ref[:, 2 * qk:] = u[:, 2 * qk:].astype(BF16)


def _gdn_prep_call(gqkv, w, ones_blk, tpb, n_ctx_tiles):
    m, c = gqkv.shape
    nt = m // TM
    hb = TM // GDN_HALO
    last = m // GDN_HALO - 1
    wp = jnp.zeros((8, c), F32).at[:SHORT_K].set(w)
    return pl.pallas_call(
        functools.partial(_gdn_prep_kernel, tpb=tpb, n_ctx_tiles=n_ctx_tiles),
        grid=(nt,),
        in_specs=[pl.BlockSpec((GDN_HALO, c), lambda t: (jnp.maximum(t * hb - 1, 0), 0)),
                  pl.BlockSpec((TM, c), lambda t: (t, 0)),
                  pl.BlockSpec((GDN_HALO, c), lambda t: (jnp.minimum((t + 1) * hb, last), 0)),
                  _const_spec(wp.shape), _const_spec(ones_blk.shape)],
        out_specs=pl.BlockSpec((TM, c), lambda t: (t, 0)),
        out_shape=jax.ShapeDtypeStruct((m, c), BF16),
        scratch_shapes=[pltpu.VMEM((TM + 2 * GDN_HALO, c), F32)],
        compiler_params=_cparams(),
        name="gdn_prep",
    )(gqkv, gqkv, gqkv, wp, ones_blk)


def _gdn_chunk(d, ci_chunk, qkv_ref, bg_ref, bgt_ref, s_ref, o_ref):
    qk = GDN_HEADS * GDN_DK
    rows = pl.ds(pl.multiple_of(ci_chunk * CHUNK, CHUNK), CHUNK)
    ri = lax.broadcasted_iota(jnp.int32, (CHUNK, CHUNK), 0)
    ci = lax.broadcasted_iota(jnp.int32, (CHUNK, CHUNK), 1)
    if d == 0:
        incl, strict, incl_t = ri >= ci, ri > ci, ri <= ci
    else:
        incl, strict, incl_t = ri <= ci, ri < ci, ri >= ci
    m_incl = incl.astype(F32)
    m_incl_t = incl_t.astype(F32)
    eye = (ri == ci).astype(F32)

    bg = bg_ref[rows, :]
    bgt = bgt_ref[ci_chunk]
    nh2 = 2 * GDN_HEADS
    gam = _hdot(m_incl, bg)
    gam_t = _hdot(bgt, m_incl_t)
    gtot = jnp.sum(bg, axis=0, keepdims=True)

    q_all = qkv_ref[rows, 0:qk]
    k_all = qkv_ref[rows, qk:2 * qk]
    v_all = qkv_ref[rows, 2 * qk:3 * qk]
    outs = []
    for h in range(GDN_HEADS):
        col = d * GDN_HEADS + h
        hs = slice(h * GDN_DK, (h + 1) * GDN_DK)
        bcol = bg[:, col:col + 1]
        gcol = gam[:, nh2 + col:nh2 + col + 1]
        grow = gam_t[nh2 + col:nh2 + col + 1, :]
        gt = gtot[:, nh2 + col:nh2 + col + 1]
        decay = jnp.where(incl, jnp.exp(jnp.where(incl, gcol - grow, 0.0)), 0.0)
        kh = k_all[:, hs]
        qh = q_all[:, hs]
        vh = v_all[:, hs]
        kk = _dot_nt(kh, kh)
        qkm = _dot_nt(qh, kh)
        a_neg = -jnp.where(strict, bcol * decay * kk, 0.0)
        t_inv = eye + a_neg
        pw = a_neg
        for _ in range(5):
            pw = _hdot(pw, pw)
            t_inv = t_inv + _hdot(t_inv, pw)
        eg = jnp.exp(gcol)
        khf = kh.astype(F32)
        w = _hdot(t_inv, (bcol * eg) * khf)
        u0 = _hdot(t_inv, bcol * vh.astype(F32))
        kd = khf * jnp.exp(gt - gcol)
        qg = qh.astype(F32) * eg
        pm = decay * qkm
        s_old = s_ref[col]
        u = u0 - _bdot(w, s_old)
        outs.append(_bdot(qg, s_old) + _bdot(pm, u))
        s_ref[col] = jnp.exp(gt) * s_old + _dot_tn(kd, u)
    o_ref[rows, :] = o_ref[rows, :] + jnp.concatenate(outs, axis=-1)


def _gdn_scan_kernel(qkv_ref, bg_ref, bgt_ref, o_ref, s_ref, *, n_ctx_chunks, n_chunks):
    s_ref[...] = jnp.zeros_like(s_ref)
    o_ref[...] = jnp.zeros_like(o_ref)

    def step(i, carry):
        cf = i
        cb = jnp.where(i < n_ctx_chunks, n_ctx_chunks - 1 - i, n_chunks - 1 + n_ctx_chunks - i)
        _gdn_chunk(0, cf, qkv_ref, bg_ref, bgt_ref, s_ref, o_ref)
        _gdn_chunk(1, cb, qkv_ref, bg_ref, bgt_ref, s_ref, o_ref)
        return carry

    lax.fori_loop(0, n_chunks, step, 0)


def _gdn_scan_call(qkvn, bg, bgt, n_batch, s_len, n_ctx):
    c = qkvn.shape[1]
    n_chunks = s_len // CHUNK
    dv = GDN_HEADS * GDN_DV
    return pl.pallas_call(
        functools.partial(_gdn_scan_kernel, n_ctx_chunks=n_ctx // CHUNK, n_chunks=n_chunks),
        grid=(n_batch,),
        in_specs=[pl.BlockSpec((s_len, c), lambda b: (b, 0), pipeline_mode=pl.Buffered(1)),
                  pl.BlockSpec((s_len, LANE), lambda b: (b, 0)),
                  pl.BlockSpec((None, n_chunks, 2 * 2 * GDN_HEADS, CHUNK), lambda b: (b, 0, 0, 0))],
        out_specs=pl.BlockSpec((s_len, dv), lambda b: (b, 0)),
        out_shape=jax.ShapeDtypeStruct((n_batch * s_len, dv), F32),
        scratch_shapes=[pltpu.VMEM((2 * GDN_HEADS, GDN_DK, GDN_DV), F32)],
        compiler_params=_cparams(),
        name="gdn_scan",
    )(qkvn, bg, bgt)


def _merge_kernel(za_ref, zb_ref, oc_ref, gg_ref, mg_ref, x_ref, mod_ref, wa_ref, wb_ref, wc_ref, wo_ref,
                  ones_ref, par_ref, go_ref, o_ref, *, alpha):
    oc = oc_ref[...]
    ms = _group_sumsq(oc, ones_ref[...]) * (1.0 / GDN_DV)
    zc = oc * lax.rsqrt(ms + EPS) * go_ref[...] * _silu(gg_ref[...])
    ya = jnp.dot(za_ref[...], wa_ref[...], preferred_element_type=F32)
    yb = jnp.dot(zb_ref[...], wb_ref[...], preferred_element_type=F32)
    yc = _bdot(zc, wc_ref[...])
    d = D_MODEL
    mix = (mg_ref[:, 0:d].astype(F32) * ya + mg_ref[:, d:2 * d].astype(F32) * yb
           + mg_ref[:, 2 * d:3 * d].astype(F32) * yc)
    y = _bdot(mix, wo_ref[...])
    g1 = mod_ref[0, 2:3, :]
    o_ref[...] = _layernorm(alpha * x_ref[...] + g1 * y, par_ref[0:1, :], par_ref[1:2, :])


def _merge_call(za, zb, oc, gg, mg, xcat, modrows, wa, wb, wc, wo, ones_blk, par, go, alpha, tpb, n_ctx_tiles):
    m, d = xcat.shape
    nt = m // TM
    n_batch = modrows.shape[0] - 1

    def mod_map(t):
        return (jnp.where(t % tpb < n_ctx_tiles, n_batch, t // tpb), 0, 0)

    row = lambda w: pl.BlockSpec((TM, w), lambda t: (t, 0))
    return pl.pallas_call(
        functools.partial(_merge_kernel, alpha=alpha),
        grid=(nt,),
        in_specs=[row(za.shape[1]), row(zb.shape[1]), row(oc.shape[1]), row(gg.shape[1]), row(mg.shape[1]), row(d),
                  pl.BlockSpec((1, 6, d), mod_map),
                  _const_spec(wa.shape), _const_spec(wb.shape), _const_spec(wc.shape), _const_spec(wo.shape),
                  _const_spec(ones_blk.shape), _const_spec(par.shape), _const_spec(go.shape)],
        out_specs=row(d),
        out_shape=jax.ShapeDtypeStruct((m, d), F32),
        compiler_params=_cparams(),
        name="merge",
    )(za, zb, oc, gg, mg, xcat, modrows, wa, wb, wc, wo, ones_blk, par, go)


def _router_kernel(x_ref, mod_ref, w_ref, b_ref, h_ref, r_ref):
    sh = mod_ref[0, 3:4, :]
    sc = mod_ref[0, 4:5, :]
    h = x_ref[...] * (1.0 + sc) + sh
    h_ref[...] = h
    lg = _hdot(h, w_ref[...]) + b_ref[...]
    lane = lax.broadcasted_iota(jnp.int32, lg.shape, 1)
    neg = jnp.float32(-1e30)
    big = jnp.int32(1 << 20)
    is_g = lane < N_GROUPS
    gmax = jnp.max(jnp.where(is_g, lg, neg), axis=-1, keepdims=True)
    gsum = jnp.sum(jnp.where(is_g, jnp.exp(lg - gmax), 0.0), axis=-1, keepdims=True)
    gp = 1.0 / gsum
    gi = jnp.min(jnp.where(jnp.logical_and(is_g, lg == gmax), lane, big), axis=-1, keepdims=True)
    lo = N_GROUPS + gi * EXP_PER_GROUP
    in_grp = jnp.logical_and(lane >= lo, lane < lo + EXP_PER_GROUP)
    v1 = jnp.max(jnp.where(in_grp, lg, neg), axis=-1, keepdims=True)
    l1 = jnp.min(jnp.where(jnp.logical_and(in_grp, lg == v1), lane, big), axis=-1, keepdims=True)
    rest = jnp.logical_and(in_grp, lane != l1)
    v2 = jnp.max(jnp.where(rest, lg, neg), axis=-1, keepdims=True)
    l2 = jnp.min(jnp.where(jnp.logical_and(rest, lg == v2), lane, big), axis=-1, keepdims=True)
    e2 = jnp.exp(v2 - v1)
    p1 = 1.0 / (1.0 + e2)
    p2 = e2 / (1.0 + e2)
    out = jnp.where(lane == 0, gp * p1, 0.0)
    out = jnp.where(lane == 1, gp * p2, out)
    out = jnp.where(lane == 2, (l1 - N_GROUPS).astype(F32), out)
    out = jnp.where(lane == 3, (l2 - N_GROUPS).astype(F32), out)
    r_ref[...] = out


def _router_call(x1, modrows, wr, br, tpb, n_ctx_tiles):
    m, d = x1.shape
    nt = m // TM
    n_batch = modrows.shape[0] - 1

    def mod_map(t):
        return (jnp.where(t % tpb < n_ctx_tiles, n_batch, t // tpb), 0, 0)

    return pl.pallas_call(
        _router_kernel,
        grid=(nt,),
        in_specs=[pl.BlockSpec((TM, d), lambda t: (t, 0)), pl.BlockSpec((1, 6, d), mod_map),
                  _const_spec(wr.shape), _const_spec(br.shape)],
        out_specs=[pl.BlockSpec((TM, d), lambda t: (t, 0)), pl.BlockSpec((TM, LANE), lambda t: (t, 0))],
        out_shape=[jax.ShapeDtypeStruct((m, d), F32), jax.ShapeDtypeStruct((m, LANE), F32)],
        compiler_params=_cparams(),
        name="router",
    )(x1, modrows, wr, br)


def _row_copy(src_hbm, dst, sem, row, j):
    return pltpu.make_async_copy(src_hbm.at[pl.ds(row, 1), :], dst.at[pl.ds(j, 1), :], sem)


def _start_gather(src_hbm, idx_ref, buf, sem, slot, n_rows):
    def body(j, carry):
        _row_copy(src_hbm, buf.at[slot], sem.at[slot], idx_ref[0, 0, j], j).start()
        return carry
    lax.fori_loop(0, n_rows, body, 0, unroll=8)


def _wait_gather(src_hbm, buf, sem, slot, n_rows):
    def body(j, carry):
        _row_copy(src_hbm, buf.at[slot], sem.at[slot], 0, j).wait()
        return carry
    lax.fori_loop(0, n_rows, body, 0, unroll=8)


def _pipelined_gather(src_hbm, idx_ref, idx_next_ref, buf, sem, n_rows):
    i = pl.program_id(0)
    n = pl.num_programs(0)
    slot = i % 2

    @pl.when(i == 0)
    def _():
        _start_gather(src_hbm, idx_ref, buf, sem, 0, n_rows)

    @pl.when(i + 1 < n)
    def _():
        _start_gather(src_hbm, idx_next_ref, buf, sem, 1 - slot, n_rows)

    _wait_gather(src_hbm, buf, sem, slot, n_rows)
    return slot


def _expert_kernel(blk_exp_ref, tok_ref, tok_next_ref, h_hbm, w13_ref, w2_ref, o_ref, buf, sem):
    slot = _pipelined_gather(h_hbm, tok_ref, tok_next_ref, buf, sem, MOE_ROWS)
    x = buf[slot].astype(BF16)
    y = jnp.dot(x, w13_ref[0], preferred_element_type=F32)
    act = (_silu(y[:, :D_EXPERT]) * y[:, D_EXPERT:]).astype(BF16)
    o_ref[...] = jnp.dot(act, w2_ref[0], preferred_element_type=F32)


def _expert_call(blk_exp, tok_buf, h2, w13, w2):
    n_blk = blk_exp.shape[0]
    d = h2.shape[1]
    tok3 = tok_buf.reshape(n_blk, 1, MOE_ROWS)
    grid_spec = pltpu.PrefetchScalarGridSpec(
        num_scalar_prefetch=1,
        grid=(n_blk,),
        in_specs=[pl.BlockSpec((1, 1, MOE_ROWS), lambda i, be: (i, 0, 0), memory_space=pltpu.SMEM),
                  pl.BlockSpec((1, 1, MOE_ROWS), lambda i, be: (jnp.minimum(i + 1, n_blk - 1), 0, 0),
                               memory_space=pltpu.SMEM),
                  pl.BlockSpec(memory_space=pl.ANY),
                  pl.BlockSpec((1, d, 2 * D_EXPERT), lambda i, be: (be[i], 0, 0)),
                  pl.BlockSpec((1, D_EXPERT, d), lambda i, be: (be[i], 0, 0))],
        out_specs=pl.BlockSpec((MOE_ROWS, d), lambda i, be: (i, 0)),
        scratch_shapes=[pltpu.VMEM((2, MOE_ROWS, d), F32), pltpu.SemaphoreType.DMA((2,))],
    )
    return pl.pallas_call(
        _expert_kernel,
        grid_spec=grid_spec,
        out_shape=jax.ShapeDtypeStruct((n_blk * MOE_ROWS, d), F32),
        compiler_params=_cparams(),
        name="expert_ffn",
    )(blk_exp, tok3, tok3, h2, w13, w2)


def _combine_kernel(pos_ref, pos_next_ref, y_hbm, x_ref, r_ref, mod_ref, par_ref, o_ref, buf, sem, *, alpha):
    slot = _pipelined_gather(y_hbm, pos_ref, pos_next_ref, buf, sem, 2 * TM)
    r = r_ref[...]
    y2 = r[:, 0:1] * buf[slot, 0:TM, :] + r[:, 1:2] * buf[slot, TM:2 * TM, :]
    g2 = mod_ref[0, 5:6, :]
    o_ref[...] = _layernorm(alpha * x_ref[...] + g2 * y2, par_ref[0:1, :], par_ref[1:2, :])


def _combine_call(pos, yb, x1, route, modrows, par, alpha, tpb, n_ctx_tiles):
    m, d = x1.shape
    nt = m // TM
    n_batch = modrows.shape[0] - 1
    pos3 = pos.reshape(nt, 1, 2 * TM)

    def mod_map(t):
        return (jnp.where(t % tpb < n_ctx_tiles, n_batch, t // tpb), 0, 0)

    return pl.pallas_call(
        functools.partial(_combine_kernel, alpha=alpha),
        grid=(nt,),
        in_specs=[pl.BlockSpec((1, 1, 2 * TM), lambda t: (t, 0, 0), memory_space=pltpu.SMEM),
                  pl.BlockSpec((1, 1, 2 * TM), lambda t: (jnp.minimum(t + 1, nt - 1), 0, 0), memory_space=pltpu.SMEM),
                  pl.BlockSpec(memory_space=pl.ANY),
                  pl.BlockSpec((TM, d), lambda t: (t, 0)), pl.BlockSpec((TM, LANE), lambda t: (t, 0)),
                  pl.BlockSpec((1, 6, d), mod_map), _const_spec(par.shape)],
        out_specs=pl.BlockSpec((TM, d), lambda t: (t, 0)),
        out_shape=jax.ShapeDtypeStruct((m, d), F32),
        scratch_shapes=[pltpu.VMEM((2, 2 * TM, d), F32), pltpu.SemaphoreType.DMA((2,))],
        compiler_params=_cparams(),
        name="moe_combine",
    )(pos3, pos3, yb, x1, route, modrows, par)


def _rope_rot_perm():
    nf = QK_ROPE // 4
    src = np.zeros((QK_ROPE,), np.int32)
    sign = np.zeros((QK_ROPE,), np.float32)
    for a in range(2):
        for f in range(nf):
            i0 = a * 2 * nf + f
            i1 = a * 2 * nf + nf + f
            src[i0], sign[i0] = i1, -1.0
            src[i1], sign[i1] = i0, 1.0
    return src, sign


def _rope_tables(n_lat, n_ctx):
    nf = QK_ROPE // 4
    rows = n_lat // GRID_W
    inv = ROPE_BASE ** (-jnp.arange(nf, dtype=F32) / nf)
    r = jnp.repeat(jnp.arange(rows, dtype=F32), GRID_W)
    col = jnp.tile(jnp.arange(GRID_W, dtype=F32), rows)
    ang = jnp.stack([r[:, None] * inv, col[:, None] * inv], axis=1)
    cos = jnp.broadcast_to(jnp.cos(ang)[:, :, None, :], (n_lat, 2, 2, nf)).reshape(n_lat, QK_ROPE)
    sin = jnp.broadcast_to(jnp.sin(ang)[:, :, None, :], (n_lat, 2, 2, nf)).reshape(n_lat, QK_ROPE)
    cos = jnp.concatenate([jnp.ones((n_ctx, QK_ROPE), F32), cos], 0)
    sin = jnp.concatenate([jnp.zeros((n_ctx, QK_ROPE), F32), sin], 0)
    s = n_lat + n_ctx
    z = lambda w: jnp.zeros((s, w), F32)
    pad = LANE - QK_NOPE - QK_ROPE
    tqc = jnp.concatenate([jnp.full((s, QK_NOPE), MLA_SCALE, F32), cos * MLA_SCALE, z(pad)], 1)
    tqs = jnp.concatenate([z(QK_NOPE), sin * MLA_SCALE, z(pad)], 1)
    tkc = jnp.concatenate([z(QK_NOPE), cos, z(pad)], 1)
    tks = jnp.concatenate([z(QK_NOPE), sin, z(pad)], 1)
    return tqc, tqs, tkc, tks


def _head_pad_cols(w, width):
    k = w.shape[0]
    w = w.reshape(k, MLA_HEADS, width)
    return jnp.pad(w, ((0, 0), (0, 0), (0, HEAD_PAD - width))).reshape(k, MLA_HEADS * HEAD_PAD)


def _layer_weights(l, w_in, b_in, conv_a_w, conv_a_b, ln_a_g, ln_a_b, w_a_out, g_q, g_kv, w_uq, w_qr, w_uk, w_uv,
                   w_b_out, conv_c_w, a_log, dt_bias, g_o, w_c_out, w_out, ln1_g, ln1_b, w_rg, b_rg, w_re, b_re,
                   w1, w3, w2, ln2_g, ln2_b):
    src, sign = _rope_rot_perm()
    wi, bi = w_in[l], b_in[l][None, :]
    sizes = (2 * CONV_CH, Q_RANK, KV_RANK, QK_ROPE, 512, 512, 512, 512, 16, 16, 3 * D_MODEL)
    offs = np.concatenate([[0], np.cumsum(sizes)])

    def cols(a, i, j=None):
        return a[:, offs[i]:offs[(i if j is None else j) + 1]]

    def regroup(a):
        kr = cols(a, 3)
        z = lambda w: jnp.zeros((a.shape[0], w), a.dtype)
        kr_grp = jnp.concatenate([z(QK_NOPE), kr, z(32), z(QK_NOPE), kr[:, src] * sign, z(32)], 1)
        ba_grp = jnp.concatenate([cols(a, 8, 9), z(LANE - 32)], 1)
        return jnp.concatenate([cols(a, 0), cols(a, 1), cols(a, 2), kr_grp, cols(a, 4, 6), cols(a, 7), ba_grp,
                                cols(a, 10)], 1)

    w_all = regroup(wi).astype(BF16)
    b_all = regroup(bi)
    zl = jnp.zeros((LANE,), F32)
    gpar = jnp.stack([zl.at[16:32].set(a_log[l].reshape(-1)), zl.at[16:32].set(dt_bias[l].reshape(-1))], 0)
    gpar = jnp.concatenate([gpar, jnp.zeros((6, LANE), F32)], 0)

    wq = jnp.concatenate([w_uq[l].reshape(Q_RANK, MLA_HEADS, QK_NOPE), w_qr[l].reshape(Q_RANK, MLA_HEADS, QK_ROPE)], 2)
    wq = _head_pad_cols(wq.reshape(Q_RANK, -1), QK_NOPE + QK_ROPE).astype(BF16)
    wqr_rot = (w_qr[l].reshape(Q_RANK, MLA_HEADS, QK_ROPE)[:, :, src] * sign)
    wqr = jnp.concatenate([jnp.zeros((Q_RANK, MLA_HEADS, QK_NOPE), F32), wqr_rot], 2)
    wqr = _head_pad_cols(wqr.reshape(Q_RANK, -1), QK_NOPE + QK_ROPE).astype(BF16)
    wk = _head_pad_cols(w_uk[l], QK_NOPE).astype(BF16)
    wv = _head_pad_cols(w_uv[l], V_DIM).astype(BF16)
    vone = jnp.zeros((MLA_HEADS, HEAD_PAD), F32).at[:, V_DIM].set(1.0).reshape(1, -1)
    wb = jnp.pad(w_b_out[l].reshape(MLA_HEADS, V_DIM, D_MODEL), ((0, 0), (0, HEAD_PAD - V_DIM), (0, 0)))
    wb = wb.reshape(MLA_HEADS * HEAD_PAD, D_MODEL).astype(BF16)

    wr = jnp.concatenate([w_rg[l], w_re[l], jnp.zeros((D_MODEL, LANE - N_GROUPS - N_EXPERTS), F32)], 1)
    br = jnp.concatenate([b_rg[l], b_re[l], jnp.zeros((LANE - N_GROUPS - N_EXPERTS,), F32)])[None, :]
    return dict(
        w_all=w_all, b_all=b_all, gpar=gpar,
        conv_a_w=conv_a_w[l], conv_a_par=jnp.stack([conv_a_b[l], ln_a_g[l], ln_a_b[l]] + [jnp.zeros_like(ln_a_b[l])] * 5, 0),
        gq=g_q[l][None, :], gkv=g_kv[l][None, :], wq=wq, wqr=wqr, wk=wk, wv=wv, vone=vone,
        conv_c_w=conv_c_w[l],
        wa=w_a_out[l].astype(BF16), wb=wb, wc=w_c_out[l].astype(BF16), wo=w_out[l].astype(BF16),
        go=jnp.tile(g_o[l], GDN_HEADS)[None, :],
        ln1=jnp.stack([ln1_g[l], ln1_b[l]] + [jnp.zeros_like(ln1_g[l])] * 6, 0),
        ln2=jnp.stack([ln2_g[l], ln2_b[l]] + [jnp.zeros_like(ln2_g[l])] * 6, 0),
        wr=wr, br=br,
        w13=jnp.concatenate([w1[l], w3[l]], -1).astype(BF16), w2=w2[l].astype(BF16),
    )


def _routing_tables(route, n_tok):
    eid = route[:, 2:4].astype(jnp.int32).reshape(-1)
    n_as = 2 * n_tok
    n_blk = n_as // MOE_ROWS + N_EXPERTS
    order = jnp.argsort(eid)
    es = eid[order]
    sizes = jnp.bincount(eid, length=N_EXPERTS).astype(jnp.int32)
    psizes = (sizes + MOE_ROWS - 1) // MOE_ROWS * MOE_ROWS
    starts = jnp.cumsum(sizes) - sizes
    pends = jnp.cumsum(psizes)
    pstarts = pends - psizes
    blk_exp = jnp.minimum(jnp.searchsorted(pends, jnp.arange(n_blk, dtype=jnp.int32) * MOE_ROWS, side="right"),
                          N_EXPERTS - 1).astype(jnp.int32)
    prow = jnp.arange(n_blk * MOE_ROWS, dtype=jnp.int32)
    pe = jnp.repeat(blk_exp, MOE_ROWS)
    j = prow - pstarts[pe]
    valid = jnp.logical_and(j >= 0, j < sizes[pe])
    src_sorted = jnp.clip(starts[pe] + j, 0, n_as - 1)
    tok_buf = jnp.where(valid, order[src_sorted] // 2, 0).astype(jnp.int32)
    dest_sorted = pstarts[es] + jnp.arange(n_as, dtype=jnp.int32) - starts[es]
    inv = jnp.argsort(order)
    pos = dest_sorted[inv].reshape(n_tok, 2)
    return blk_exp, tok_buf, pos


def kernel(x, c, ctx, c_ctx, w_mod, b_mod, w_in, b_in, conv_a_w, conv_a_b, ln_a_g, ln_a_b, w_a_out, g_q, g_kv, w_uq,
           w_qr, w_uk, w_uv, w_b_out, conv_c_w, a_log, dt_bias, g_o, w_c_out, w_out, ln1_g, ln1_b, w_rg, b_rg, w_re,
           b_re, w1, w3, w2, ln2_g, ln2_b):
    n_batch, n_lat, d = x.shape
    n_ctx = ctx.shape[1]
    depth = w_mod.shape[0]
    alpha = (2 * depth) ** 0.25
    s_len = n_ctx + n_lat
    m = n_batch * s_len
    tpb = s_len // TM
    n_ctx_tiles = n_ctx // TM
    assert n_ctx % TM == 0 and n_lat % TM == 0 and n_ctx % CHUNK == 0 and s_len % n_ctx == 0

    xcat = jnp.concatenate([ctx, x], axis=1).reshape(m, d)
    cvec = jnp.concatenate([c, c_ctx[None, :], jnp.zeros((16 - n_batch - 1, d), F32)], 0)
    tabs = _rope_tables(n_lat, n_ctx)
    ones_blk = jnp.kron(jnp.eye(GDN_HEADS, dtype=F32), jnp.ones((GDN_DK, GDN_DK), F32)).astype(BF16)
    params = (w_in, b_in, conv_a_w, conv_a_b, ln_a_g, ln_a_b, w_a_out, g_q, g_kv, w_uq, w_qr, w_uk, w_uv, w_b_out,
              conv_c_w, a_log, dt_bias, g_o, w_c_out, w_out, ln1_g, ln1_b, w_rg, b_rg, w_re, b_re, w1, w3, w2,
              ln2_g, ln2_b)
    tq = TM

    for l in range(depth):
        p = _layer_weights(l, *params)
        mod = _mod_call(cvec, w_mod[l].astype(BF16), b_mod[l][None, :])
        modrows = mod[:n_batch + 1].reshape(n_batch + 1, 6, d)

        u, qd, kvd, kr, gqkv, gg, ba, mg = _inproj_call(xcat, modrows, p["w_all"], p["b_all"], p["gpar"], tpb,
                                                        n_ctx_tiles)
        za = _conv_a_call(u, p["conv_a_w"], p["conv_a_par"], tpb, n_ctx_tiles)
        qp, kp, vp = _mla_proj_call(qd, kvd, kr, tabs, p["gq"], p["gkv"], p["wq"], p["wqr"], p["wk"], p["wv"],
                                    p["vone"], tpb)
        zb_l = _attn_call(qp, kp, vp, n_batch, s_len, n_ctx, n_lat, s_len, tq)
        zb_c = _attn_call(qp, kp, vp, n_batch, s_len, 0, n_ctx, n_ctx, n_ctx)
        hp = MLA_HEADS * HEAD_PAD
        zb = jnp.concatenate([zb_c.reshape(n_batch, n_ctx, hp), zb_l.reshape(n_batch, n_lat, hp)], 1).reshape(m, hp)
        qkvn = _gdn_prep_call(gqkv, p["conv_c_w"], ones_blk, tpb, n_ctx_tiles)
        bgt = ba[:, :32].reshape(n_batch, s_len // CHUNK, CHUNK, 32).transpose(0, 1, 3, 2)
        oc = _gdn_scan_call(qkvn, ba, bgt, n_batch, s_len, n_ctx)
        x1 = _merge_call(za, zb, oc, gg, mg, xcat, modrows, p["wa"], p["wb"], p["wc"], p["wo"], ones_blk, p["ln1"],
                         p["go"], alpha, tpb, n_ctx_tiles)
        h2, route = _router_call(x1, modrows, p["wr"], p["br"], tpb, n_ctx_tiles)
        blk_exp, tok_buf, pos = _routing_tables(route, m)
        yb = _expert_call(blk_exp, tok_buf, h2, p["w13"], p["w2"])
        pos_t = pos.reshape(m // TM, TM, 2).transpose(0, 2, 1).reshape(-1)
        xcat = _combine_call(pos_t, yb, x1, route, modrows, p["ln2"], alpha, tpb, n_ctx_tiles)

    return xcat.reshape(n_batch, s_len, d)[:, n_ctx:, :]
```

```python
import functools

import jax
import jax.numpy as jnp
import numpy as np
from jax import lax
from jax.experimental import pallas as pl
from jax.experimental.pallas import tpu as pltpu

F32 = jnp.float32
BF16 = jnp.bfloat16

D_MODEL = 1024
GRID_W = 64
EPS = 1e-6
CONV_CH = 512
CONV_K = 31
MLA_HEADS = 8
QK_NOPE = 64
QK_ROPE = 32
V_DIM = 64
Q_RANK = 384
KV_RANK = 256
ROPE_BASE = 10000.0
MLA_SCALE = (QK_NOPE + QK_ROPE) ** -0.5
GDN_HEADS = 8
GDN_DK = 64
GDN_DV = 64
SHORT_K = 5
CHUNK = 64
N_GROUPS = 4
EXP_PER_GROUP = 8
N_EXPERTS = N_GROUPS * EXP_PER_GROUP
D_EXPERT = 256

LANE = 128
TM = 256
HEAD_PAD = LANE
MOE_ROWS = 256
VMEM_LIMIT = 56 * 1024 * 1024

OFF_GLU = 0
OFF_QD = OFF_GLU + 2 * CONV_CH
OFF_KVD = OFF_QD + Q_RANK
OFF_KR = OFF_KVD + KV_RANK
OFF_GQKV = OFF_KR + 2 * LANE
OFF_GG = OFF_GQKV + 3 * GDN_HEADS * GDN_DK
OFF_BA = OFF_GG + GDN_HEADS * GDN_DV
OFF_MG = OFF_BA + LANE
P_ALL = OFF_MG + 3 * D_MODEL


def _cparams(n_grid=1):
    return pltpu.CompilerParams(dimension_semantics=("arbitrary",) * n_grid, vmem_limit_bytes=VMEM_LIMIT)


def _const_spec(shape):
    nd = len(shape)
    return pl.BlockSpec(shape, lambda *_: (0,) * nd, pipeline_mode=pl.Buffered(1))


def _bdot(a, b):
    return jnp.dot(a.astype(BF16), b.astype(BF16), preferred_element_type=F32)


def _hdot(a, b):
    return jnp.dot(a, b, precision=lax.Precision.HIGHEST, preferred_element_type=F32)


def _dot_nt(a, b):
    return lax.dot_general(a.astype(BF16), b.astype(BF16), (((1,), (1,)), ((), ())), preferred_element_type=F32)


def _dot_tn(a, b):
    return lax.dot_general(a.astype(BF16), b.astype(BF16), (((0,), (0,)), ((), ())), preferred_element_type=F32)


def _silu(x):
    return x * jax.nn.sigmoid(x)


def _group_sumsq(x, ones_blk):
    x2 = x * x
    hi = x2.astype(BF16)
    lo = (x2 - hi.astype(F32)).astype(BF16)
    return jnp.dot(hi, ones_blk, preferred_element_type=F32) + jnp.dot(lo, ones_blk, preferred_element_type=F32)


def _layernorm(x, g, b):
    mu = jnp.mean(x, -1, keepdims=True)
    xc = x - mu
    var = jnp.mean(xc * xc, -1, keepdims=True)
    return xc * lax.rsqrt(var + EPS) * g + b


def _mod_kernel(c_ref, w_ref, b_ref, o_ref):
    o_ref[...] = _bdot(_silu(c_ref[...]), w_ref[...]) + b_ref[...]


def _mod_call(cvec, w, b):
    rows, d = cvec.shape
    n = w.shape[1]
    tn = 1536
    return pl.pallas_call(
        _mod_kernel,
        grid=(n // tn,),
        in_specs=[pl.BlockSpec((rows, d), lambda j: (0, 0)),
                  pl.BlockSpec((d, tn), lambda j: (0, j)),
                  pl.BlockSpec((1, tn), lambda j: (0, j))],
        out_specs=pl.BlockSpec((rows, tn), lambda j: (0, j)),
        out_shape=jax.ShapeDtypeStruct((rows, n), F32),
        compiler_params=_cparams(),
        name="mod_vectors",
    )(cvec, w, b)


def _inproj_kernel(x_ref, mod_ref, w_ref, b_ref, gpar_ref,
                   u_ref, qd_ref, kvd_ref, kr_ref, gqkv_ref, gg_ref, ba_ref, mg_ref):
    x = x_ref[...]
    sh = mod_ref[0, 0:1, :]
    sc = mod_ref[0, 1:2, :]
    h = (x * (1.0 + sc) + sh).astype(BF16)

    def proj(off, n):
        return jnp.dot(h, w_ref[:, off:off + n], preferred_element_type=F32) + b_ref[:, off:off + n]

    y = proj(OFF_GLU, 2 * CONV_CH)
    u_ref[...] = y[:, :CONV_CH] * jax.nn.sigmoid(y[:, CONV_CH:])
    qd_ref[...] = proj(OFF_QD, Q_RANK)
    kvd_ref[...] = proj(OFF_KVD, KV_RANK)
    kr_ref[...] = proj(OFF_KR, 2 * LANE)
    gqkv_ref[...] = proj(OFF_GQKV, 3 * GDN_HEADS * GDN_DK)
    gg_ref[...] = proj(OFF_GG, GDN_HEADS * GDN_DV)
    y = proj(OFF_BA, LANE)
    lane = lax.broadcasted_iota(jnp.int32, y.shape, 1)
    z = y + gpar_ref[1:2, :]
    sp = jnp.maximum(z, 0.0) + jnp.log1p(jnp.exp(-jnp.abs(z)))
    ba_ref[...] = jnp.where(lane < 2 * GDN_HEADS, jax.nn.sigmoid(y), -jnp.exp(gpar_ref[0:1, :]) * sp)
    mg_ref[...] = jax.nn.sigmoid(proj(OFF_MG, 3 * D_MODEL)).astype(BF16)


def _inproj_call(xcat, modrows, w_all, b_all, gpar, tpb, n_ctx_tiles):
    m, d = xcat.shape
    nt = m // TM
    n_batch = modrows.shape[0] - 1

    def mod_map(t):
        return (jnp.where(t % tpb < n_ctx_tiles, n_batch, t // tpb), 0, 0)

    widths = (CONV_CH, Q_RANK, KV_RANK, 2 * LANE, 3 * GDN_HEADS * GDN_DK, GDN_HEADS * GDN_DV, LANE, 3 * D_MODEL)
    dtypes = (F32, F32, F32, F32, F32, F32, F32, BF16)
    return pl.pallas_call(
        _inproj_kernel,
        grid=(nt,),
        in_specs=[pl.BlockSpec((TM, d), lambda t: (t, 0)),
                  pl.BlockSpec((1, 6, d), mod_map),
                  _const_spec(w_all.shape), _const_spec(b_all.shape), _const_spec(gpar.shape)],
        out_specs=[pl.BlockSpec((TM, w), lambda t: (t, 0)) for w in widths],
        out_shape=[jax.ShapeDtypeStruct((m, w), dt) for w, dt in zip(widths, dtypes)],
        compiler_params=_cparams(),
        name="in_proj",
    )(xcat, modrows, w_all, b_all, gpar)


CONV_HALO = 16


def _conv_a_kernel(top_ref, cur_ref, bot_ref, w_ref, par_ref, o_ref, win_ref, *, tpb, n_ctx_tiles):
    t = pl.program_id(0)
    r = t % tpb
    top_ok = jnp.logical_and(r != 0, r != n_ctx_tiles)
    bot_ok = jnp.logical_and(r != n_ctx_tiles - 1, r != tpb - 1)
    win_ref[0:CONV_HALO, :] = jnp.where(top_ok, top_ref[...], 0.0)
    win_ref[CONV_HALO:CONV_HALO + TM, :] = cur_ref[...]
    win_ref[CONV_HALO + TM:, :] = jnp.where(bot_ok, bot_ref[...], 0.0)
    acc = jnp.zeros((TM, CONV_CH), F32)
    base = CONV_HALO - CONV_K // 2
    for k in range(CONV_K):
        acc = acc + win_ref[pl.ds(base + k, TM), :] * w_ref[k:k + 1, :]
    y = _layernorm(acc + par_ref[0:1, :], par_ref[1:2, :], par_ref[2:3, :])
    o_ref[...] = _silu(y).astype(BF16)


def _conv_a_call(u, w, par, tpb, n_ctx_tiles):
    m, c = u.shape
    nt = m // TM
    hb = TM // CONV_HALO
    last = m // CONV_HALO - 1
    wp = jnp.zeros((32, c), F32).at[:CONV_K].set(w)
    return pl.pallas_call(
        functools.partial(_conv_a_kernel, tpb=tpb, n_ctx_tiles=n_ctx_tiles),
        grid=(nt,),
        in_specs=[pl.BlockSpec((CONV_HALO, c), lambda t: (jnp.maximum(t * hb - 1, 0), 0)),
                  pl.BlockSpec((TM, c), lambda t: (t, 0)),
                  pl.BlockSpec((CONV_HALO, c), lambda t: (jnp.minimum((t + 1) * hb, last), 0)),
                  _const_spec(wp.shape), _const_spec(par.shape)],
        out_specs=pl.BlockSpec((TM, c), lambda t: (t, 0)),
        out_shape=jax.ShapeDtypeStruct((m, c), BF16),
        scratch_shapes=[pltpu.VMEM((TM + 2 * CONV_HALO, c), F32)],
        compiler_params=_cparams(),
        name="conv_branch",
    )(u, u, u, wp, par)


def _mla_proj_kernel(qd_ref, kvd_ref, kr_ref, tqc_ref, tqs_ref, tkc_ref, tks_ref, gq_ref, gkv_ref,
                     wq_ref, wqr_ref, wk_ref, wv_ref, vone_ref, q_ref, k_ref, v_ref):
    qd = qd_ref[...]
    cq = qd * lax.rsqrt(jnp.mean(qd * qd, -1, keepdims=True) + EPS) * gq_ref[...]
    qf = _bdot(cq, wq_ref[...])
    qr = _bdot(cq, wqr_ref[...])
    kvd = kvd_ref[...]
    ckv = kvd * lax.rsqrt(jnp.mean(kvd * kvd, -1, keepdims=True) + EPS) * gkv_ref[...]
    kn = _bdot(ckv, wk_ref[...])
    krf = kr_ref[:, :LANE] * tkc_ref[...] + kr_ref[:, LANE:] * tks_ref[...]
    tqc = tqc_ref[...]
    tqs = tqs_ref[...]
    for h in range(MLA_HEADS):
        sl = slice(h * HEAD_PAD, (h + 1) * HEAD_PAD)
        q_ref[:, sl] = (qf[:, sl] * tqc + qr[:, sl] * tqs).astype(BF16)
        k_ref[:, sl] = (kn[:, sl] + krf).astype(BF16)
    v_ref[...] = (_bdot(ckv, wv_ref[...]) + vone_ref[...]).astype(BF16)


def _mla_proj_call(qd, kvd, kr, tabs, gq, gkv, wq, wqr, wk, wv, vone, tpb):
    m = qd.shape[0]
    nt = m // TM
    hp = MLA_HEADS * HEAD_PAD
    tab_spec = pl.BlockSpec((TM, LANE), lambda t: (t % tpb, 0))
    row = lambda w: pl.BlockSpec((TM, w), lambda t: (t, 0))
    return pl.pallas_call(
        _mla_proj_kernel,
        grid=(nt,),
        in_specs=[row(Q_RANK), row(KV_RANK), row(2 * LANE), tab_spec, tab_spec, tab_spec, tab_spec,
                  _const_spec(gq.shape), _const_spec(gkv.shape), _const_spec(wq.shape), _const_spec(wqr.shape),
                  _const_spec(wk.shape), _const_spec(wv.shape), _const_spec(vone.shape)],
        out_specs=[row(hp), row(hp), row(hp)],
        out_shape=[jax.ShapeDtypeStruct((m, hp), BF16)] * 3,
        compiler_params=_cparams(),
        name="mla_proj",
    )(qd, kvd, kr, *tabs, gq, gkv, wq, wqr, wk, wv, vone)


def _attn_kernel(q_ref, k_ref, v_ref, o_ref):
    for h in range(MLA_HEADS):
        sl = slice(h * HEAD_PAD, (h + 1) * HEAD_PAD)
        s = lax.dot_general(q_ref[:, sl], k_ref[:, sl], (((1,), (1,)), ((), ())), preferred_element_type=F32)
        m = jnp.max(s, axis=-1, keepdims=True)
        p = jnp.exp(s - m).astype(BF16)
        o = jnp.dot(p, v_ref[:, sl], preferred_element_type=F32)
        denom = o[:, V_DIM:V_DIM + 1]
        o_ref[:, sl] = (o * (1.0 / denom)).astype(BF16)


def _attn_call(q, k, v, n_batch, s_len, q_start, q_len, k_len, tq):
    hp = MLA_HEADS * HEAD_PAD
    nq = q_len // tq
    q0 = q_start // tq
    spb = s_len // tq
    kb = s_len // k_len
    return pl.pallas_call(
        _attn_kernel,
        grid=(n_batch, nq),
        in_specs=[pl.BlockSpec((tq, hp), lambda b, i: (b * spb + q0 + i, 0)),
                  pl.BlockSpec((k_len, hp), lambda b, i: (b * kb, 0), pipeline_mode=pl.Buffered(1)),
                  pl.BlockSpec((k_len, hp), lambda b, i: (b * kb, 0), pipeline_mode=pl.Buffered(1))],
        out_specs=pl.BlockSpec((tq, hp), lambda b, i: (b * nq + i, 0)),
        out_shape=jax.ShapeDtypeStruct((n_batch * q_len, hp), BF16),
        compiler_params=_cparams(2),
        name="attention",
    )(q, k, v)


GDN_HALO = 8


def _gdn_prep_kernel(top_ref, cur_ref, bot_ref, w_ref, ones_ref, o_ref, win_ref, *, tpb, n_ctx_tiles):
    t = pl.program_id(0)
    r = t % tpb
    top_ok = jnp.logical_and(r != 0, r != n_ctx_tiles)
    bot_ok = jnp.logical_and(r != n_ctx_tiles - 1, r != tpb - 1)
    win_ref[0:GDN_HALO, :] = jnp.where(top_ok, top_ref[...], 0.0)
    win_ref[GDN_HALO:GDN_HALO + TM, :] = cur_ref[...]
    win_ref[GDN_HALO + TM:, :] = jnp.where(bot_ok, bot_ref[...], 0.0)
    c = 3 * GDN_HEADS * GDN_DK
    acc = jnp.zeros((TM, c), F32)
    base = GDN_HALO - SHORT_K // 2
    for k in range(SHORT_K):
        acc = acc + win_ref[pl.ds(base + k, TM), :] * w_ref[k:k + 1, :]
    u = _silu(acc)
    qk = GDN_HEADS * GDN_DK
    q = u[:, :qk]
    kk = u[:, qk:2 * qk]
    ones_blk = ones_ref[...]
    q = q * lax.rsqrt(_group_sumsq(q, ones_blk) + EPS) * (GDN_DK ** -0.5)
    kk = kk * lax.rsqrt(_group_sumsq(kk, ones_blk) + EPS)
    o_ref[:, :qk] = q.astype(BF16)
    o_ref[:, qk:2 * qk] = kk.astype(BF16)
    o_ref[:, 2 * qk:] = u[:, 2 * qk:].astype(BF16)


def _gdn_prep_call(gqkv, w, ones_blk, tpb, n_ctx_tiles):
    m, c = gqkv.shape
    nt = m // TM
    hb = TM // GDN_HALO
    last = m // GDN_HALO - 1
    wp = jnp.zeros((8, c), F32).at[:SHORT_K].set(w)
    return pl.pallas_call(
        functools.partial(_gdn_prep_kernel, tpb=tpb, n_ctx_tiles=n_ctx_tiles),
        grid=(nt,),
        in_specs=[pl.BlockSpec((GDN_HALO, c), lambda t: (jnp.maximum(t * hb - 1, 0), 0)),
                  pl.BlockSpec((TM, c), lambda t: (t, 0)),
                  pl.BlockSpec((GDN_HALO, c), lambda t: (jnp.minimum((t + 1) * hb, last), 0)),
                  _const_spec(wp.shape), _const_spec(ones_blk.shape)],
        out_specs=pl.BlockSpec((TM, c), lambda t: (t, 0)),
        out_shape=jax.ShapeDtypeStruct((m, c), BF16),
        scratch_shapes=[pltpu.VMEM((TM + 2 * GDN_HALO, c), F32)],
        compiler_params=_cparams(),
        name="gdn_prep",
    )(gqkv, gqkv, gqkv, wp, ones_blk)


N_PAIRS = GDN_HEADS // 2
CPT = TM // CHUNK


def _left_half(shape):
    return lax.broadcasted_iota(jnp.int32, shape, len(shape) - 1) < GDN_DK


def _blockdiag(x2):
    left = _left_half(x2.shape)
    zero = jnp.zeros_like(x2)
    return jnp.concatenate([jnp.where(left, x2, zero), jnp.where(left, zero, x2)], axis=0).astype(BF16)


def _lane_bcast_pair(x, c0):
    shape = (x.shape[0], LANE)
    return jnp.where(_left_half(shape), jnp.broadcast_to(x[:, c0:c0 + 1], shape),
                     jnp.broadcast_to(x[:, c0 + 1:c0 + 2], shape))


def _gdn_local_kernel(qkv_ref, bg_ref, bgt_ref, w_ref, u0_ref, kd_ref, qg_ref, pm_ref, gcx_ref):
    qk = GDN_HEADS * GDN_DK
    nh2 = 2 * GDN_HEADS
    ri = lax.broadcasted_iota(jnp.int32, (CHUNK, LANE), 0)
    ci = lax.broadcasted_iota(jnp.int32, (CHUNK, LANE), 1) % CHUNK
    incl = (ri >= ci, ri <= ci)
    strict = (ri > ci, ri < ci)
    eye2 = (ri == ci).astype(F32)
    r64 = lax.broadcasted_iota(jnp.int32, (CHUNK, CHUNK), 0)
    c64 = lax.broadcasted_iota(jnp.int32, (CHUNK, CHUNK), 1)
    m_incl = ((r64 >= c64).astype(F32), (r64 <= c64).astype(F32))
    rj = lax.broadcasted_iota(jnp.int32, (LANE, LANE), 0)
    cl = lax.broadcasted_iota(jnp.int32, (LANE, LANE), 1)
    same = (rj // CHUNK) == (cl // CHUNK)
    m_incl_t = (jnp.logical_and(same, rj % CHUNK <= cl % CHUNK).astype(F32),
                jnp.logical_and(same, rj % CHUNK >= cl % CHUNK).astype(F32))
    ones_bd = same.astype(F32)

    chains = []
    for j in range(CPT):
        rows = slice(j * CHUNK, (j + 1) * CHUNK)
        bg = bg_ref[rows, :]
        bgt = bgt_ref[j]
        gam = (_hdot(m_incl[0], bg), _hdot(m_incl[1], bg))
        gam_t = (_hdot(bgt, m_incl_t[0]), _hdot(bgt, m_incl_t[1]))
        gt_rows = _hdot(bgt, ones_bd)
        gcx_ref[j * 8:(j + 1) * 8, :] = jnp.exp(gt_rows[8:16])
        for p in range(N_PAIRS):
            ls = slice(p * LANE, (p + 1) * LANE)
            q2 = qkv_ref[rows, p * LANE:(p + 1) * LANE]
            k2 = qkv_ref[rows, qk + p * LANE:qk + (p + 1) * LANE]
            v2 = qkv_ref[rows, 2 * qk + p * LANE:2 * qk + (p + 1) * LANE]
            left = _left_half(k2.shape)
            zero = jnp.zeros_like(k2)
            k_bd = jnp.concatenate([jnp.where(left, k2, zero), jnp.where(left, zero, k2)], axis=0)
            kq = lax.dot_general(jnp.concatenate([k2, q2], axis=0), k_bd, (((1,), (1,)), ((), ())),
                                 preferred_element_type=F32)
            kk2, qk2 = kq[:CHUNK], kq[CHUNK:]
            k2f, q2f, v2f = k2.astype(F32), q2.astype(F32), v2.astype(F32)
            for d in range(2):
                c0 = d * GDN_HEADS + 2 * p
                r = 8 + d * N_PAIRS + p
                b2 = _lane_bcast_pair(bg, c0)
                gcol2 = _lane_bcast_pair(gam[d], nh2 + c0)
                grow2 = gam_t[d][r:r + 1, :]
                gt2 = gt_rows[r:r + 1, :]
                decay2 = jnp.where(incl[d], jnp.exp(jnp.where(incl[d], gcol2 - grow2, 0.0)), 0.0)
                n2 = -jnp.where(strict[d], b2 * decay2 * kk2, 0.0)
                eg2 = jnp.exp(gcol2)
                rhs = jnp.concatenate([_blockdiag((b2 * eg2) * k2f), _blockdiag(b2 * v2f)], axis=1)
                kd_ref[d, rows, ls] = (k2f * jnp.exp(gt2 - gcol2)).astype(BF16)
                qg_ref[d, rows, ls] = (q2f * eg2).astype(BF16)
                pm_ref[d, rows, ls] = (decay2 * qk2).astype(BF16)
                chains.append(dict(d=d, rows=rows, ls=ls, pw=n2, t=eye2 + n2, rhs=rhs))

    for ch in chains:
        ch["pw"] = jnp.dot(ch["pw"].astype(BF16), _blockdiag(ch["pw"]), preferred_element_type=F32)
    for _ in range(4):
        for ch in chains:
            res = jnp.dot(jnp.concatenate([ch["pw"], ch["t"]], axis=0).astype(BF16), _blockdiag(ch["pw"]),
                          preferred_element_type=F32)
            ch["pw"] = res[:CHUNK]
            ch["t"] = ch["t"] + res[CHUNK:]
    for ch in chains:
        ch["t"] = ch["t"] + jnp.dot(ch["t"].astype(BF16), _blockdiag(ch["pw"]), preferred_element_type=F32)
    for ch in chains:
        x = jnp.dot(ch["t"].astype(BF16), ch["rhs"], preferred_element_type=F32)
        w_ref[ch["d"], ch["rows"], ch["ls"]] = x[:, :LANE].astype(BF16)
        u0_ref[ch["d"], ch["rows"], ch["ls"]] = x[:, LANE:]


def _gdn_local_call(qkvn, bg, bgt2):
    m, c = qkvn.shape
    nt = m // TM
    dv = GDN_HEADS * GDN_DV
    spec2 = pl.BlockSpec((2, TM, dv), lambda t: (0, t, 0))
    shp = lambda dt: jax.ShapeDtypeStruct((2, m, dv), dt)
    return pl.pallas_call(
        _gdn_local_kernel,
        grid=(nt,),
        in_specs=[pl.BlockSpec((TM, c), lambda t: (t, 0)), pl.BlockSpec((TM, LANE), lambda t: (t, 0)),
                  pl.BlockSpec((CPT, 16, LANE), lambda t: (t, 0, 0))],
        out_specs=[spec2, spec2, spec2, spec2, spec2, pl.BlockSpec((CPT * 8, LANE), lambda t: (t, 0))],
        out_shape=[shp(BF16), shp(F32), shp(BF16), shp(BF16), shp(BF16),
                   jax.ShapeDtypeStruct((m // CHUNK * 8, LANE), F32)],
        compiler_params=_cparams(),
        name="gdn_local",
    )(qkvn, bg, bgt2)


def _gdn_scan_kernel(wf, u0f, kdf, qgf, pmf, gcf, wb, u0b, kdb, qgb, pmb, gcb, of_ref, ob_ref, s_ref):
    @pl.when(pl.program_id(1) == 0)
    def _():
        s_ref[...] = jnp.zeros_like(s_ref)

    refs = ((wf, u0f, kdf, qgf, pmf, gcf, of_ref), (wb, u0b, kdb, qgb, pmb, gcb, ob_ref))
    left = _left_half((CHUNK, LANE))
    state = [s_ref[i] for i in range(2 * N_PAIRS)]
    for sub in range(CPT):
        cur = []
        for d in range(2):
            j = sub if d == 0 else CPT - 1 - sub
            rows = slice(j * CHUNK, (j + 1) * CHUNK)
            for p in range(N_PAIRS):
                cur.append((d, p, rows, slice(p * LANE, (p + 1) * LANE), j * 8 + d * N_PAIRS + p))
        res = []
        for d, p, rows, ls, gr in cur:
            w_r, _, _, qg_r, _, _, _ = refs[d]
            lhs = jnp.concatenate([w_r[rows, ls], qg_r[rows, ls]], axis=0)
            res.append(jnp.dot(lhs, _blockdiag(state[d * N_PAIRS + p]), preferred_element_type=F32))
        us = []
        for (d, p, rows, ls, gr), r in zip(cur, res):
            us.append(refs[d][1][rows, ls] - r[:CHUNK])
        for (d, p, rows, ls, gr), r, u in zip(cur, res, us):
            _, _, kd_r, _, pm_r, gc_r, o_r = refs[d]
            o_r[rows, ls] = r[CHUNK:] + jnp.dot(pm_r[rows, ls], _blockdiag(u), preferred_element_type=F32)
            upd = lax.dot_general(kd_r[rows, ls], u.astype(BF16), (((0,), (0,)), ((), ())),
                                  preferred_element_type=F32)
            i = d * N_PAIRS + p
            state[i] = gc_r[gr:gr + 1, :] * state[i] + jnp.where(left, upd[:CHUNK], upd[CHUNK:])
    for i in range(2 * N_PAIRS):
        s_ref[i] = state[i]


def _gdn_scan_call(w, u0, kd, qg, pm, gcx, n_batch, tpb, n_ctx_tiles):
    m, dv = w.shape[1], w.shape[2]

    def fwd_tile(b, j):
        return b * tpb + j

    def bwd_tile(b, j):
        return b * tpb + jnp.where(j < n_ctx_tiles, n_ctx_tiles - 1 - j, tpb - 1 + n_ctx_tiles - j)

    def specs(d, tile):
        big = pl.BlockSpec((None, TM, dv), lambda b, j: (d, tile(b, j), 0))
        return [big, big, big, big, big, pl.BlockSpec((CPT * 8, LANE), lambda b, j: (tile(b, j), 0))]

    return pl.pallas_call(
        _gdn_scan_kernel,
        grid=(n_batch, tpb),
        in_specs=specs(0, fwd_tile) + specs(1, bwd_tile),
        out_specs=[pl.BlockSpec((TM, dv), lambda b, j: (fwd_tile(b, j), 0)),
                   pl.BlockSpec((TM, dv), lambda b, j: (bwd_tile(b, j), 0))],
        out_shape=[jax.ShapeDtypeStruct((m, dv), F32)] * 2,
        scratch_shapes=[pltpu.VMEM((2 * N_PAIRS, GDN_DK, LANE), F32)],
        compiler_params=_cparams(2),
        name="gdn_scan",
    )(w, u0, kd, qg, pm, gcx, w, u0, kd, qg, pm, gcx)


def _merge_kernel(za_ref, zb_ref, of_ref, ob_ref, gg_ref, mg_ref, x_ref, mod_ref, wa_ref, wb_ref, wc_ref, wo_ref,
                  ones_ref, par_ref, go_ref, o_ref, *, alpha):
    oc = of_ref[...] + ob_ref[...]
    ms = _group_sumsq(oc, ones_ref[...]) * (1.0 / GDN_DV)
    zc = oc * lax.rsqrt(ms + EPS) * go_ref[...] * _silu(gg_ref[...])
    ya = jnp.dot(za_ref[...], wa_ref[...], preferred_element_type=F32)
    yb = jnp.dot(zb_ref[...], wb_ref[...], preferred_element_type=F32)
    yc = _bdot(zc, wc_ref[...])
    d = D_MODEL
    mix = (mg_ref[:, 0:d].astype(F32) * ya + mg_ref[:, d:2 * d].astype(F32) * yb
           + mg_ref[:, 2 * d:3 * d].astype(F32) * yc)
    y = _bdot(mix, wo_ref[...])
    g1 = mod_ref[0, 2:3, :]
    o_ref[...] = _layernorm(alpha * x_ref[...] + g1 * y, par_ref[0:1, :], par_ref[1:2, :])


def _merge_call(za, zb, of, ob, gg, mg, xcat, modrows, wa, wb, wc, wo, ones_blk, par, go, alpha, tpb, n_ctx_tiles):
    m, d = xcat.shape
    nt = m // TM
    n_batch = modrows.shape[0] - 1

    def mod_map(t):
        return (jnp.where(t % tpb < n_ctx_tiles, n_batch, t // tpb), 0, 0)

    row = lambda w: pl.BlockSpec((TM, w), lambda t: (t, 0))
    return pl.pallas_call(
        functools.partial(_merge_kernel, alpha=alpha),
        grid=(nt,),
        in_specs=[row(za.shape[1]), row(zb.shape[1]), row(of.shape[1]), row(ob.shape[1]), row(gg.shape[1]),
                  row(mg.shape[1]), row(d),
                  pl.BlockSpec((1, 6, d), mod_map),
                  _const_spec(wa.shape), _const_spec(wb.shape), _const_spec(wc.shape), _const_spec(wo.shape),
                  _const_spec(ones_blk.shape), _const_spec(par.shape), _const_spec(go.shape)],
        out_specs=row(d),
        out_shape=jax.ShapeDtypeStruct((m, d), F32),
        compiler_params=_cparams(),
        name="merge",
    )(za, zb, of, ob, gg, mg, xcat, modrows, wa, wb, wc, wo, ones_blk, par, go)


def _router_kernel(x_ref, mod_ref, w_ref, b_ref, h_ref, r_ref):
    sh = mod_ref[0, 3:4, :]
    sc = mod_ref[0, 4:5, :]
    h = x_ref[...] * (1.0 + sc) + sh
    h_ref[...] = h
    lg = _hdot(h, w_ref[...]) + b_ref[...]
    lane = lax.broadcasted_iota(jnp.int32, lg.shape, 1)
    neg = jnp.float32(-1e30)
    big = jnp.int32(1 << 20)
    is_g = lane < N_GROUPS
    gmax = jnp.max(jnp.where(is_g, lg, neg), axis=-1, keepdims=True)
    gsum = jnp.sum(jnp.where(is_g, jnp.exp(lg - gmax), 0.0), axis=-1, keepdims=True)
    gp = 1.0 / gsum
    gi = jnp.min(jnp.where(jnp.logical_and(is_g, lg == gmax), lane, big), axis=-1, keepdims=True)
    lo = N_GROUPS + gi * EXP_PER_GROUP
    in_grp = jnp.logical_and(lane >= lo, lane < lo + EXP_PER_GROUP)
    v1 = jnp.max(jnp.where(in_grp, lg, neg), axis=-1, keepdims=True)
    l1 = jnp.min(jnp.where(jnp.logical_and(in_grp, lg == v1), lane, big), axis=-1, keepdims=True)
    rest = jnp.logical_and(in_grp, lane != l1)
    v2 = jnp.max(jnp.where(rest, lg, neg), axis=-1, keepdims=True)
    l2 = jnp.min(jnp.where(jnp.logical_and(rest, lg == v2), lane, big), axis=-1, keepdims=True)
    e2 = jnp.exp(v2 - v1)
    p1 = 1.0 / (1.0 + e2)
    p2 = e2 / (1.0 + e2)
    out = jnp.where(lane == 0, gp * p1, 0.0)
    out = jnp.where(lane == 1, gp * p2, out)
    out = jnp.where(lane == 2, (l1 - N_GROUPS).astype(F32), out)
    out = jnp.where(lane == 3, (l2 - N_GROUPS).astype(F32), out)
    r_ref[...] = out


def _router_call(x1, modrows, wr, br, tpb, n_ctx_tiles):
    m, d = x1.shape
    nt = m // TM
    n_batch = modrows.shape[0] - 1

    def mod_map(t):
        return (jnp.where(t % tpb < n_ctx_tiles, n_batch, t // tpb), 0, 0)

    return pl.pallas_call(
        _router_kernel,
        grid=(nt,),
        in_specs=[pl.BlockSpec((TM, d), lambda t: (t, 0)), pl.BlockSpec((1, 6, d), mod_map),
                  _const_spec(wr.shape), _const_spec(br.shape)],
        out_specs=[pl.BlockSpec((TM, d), lambda t: (t, 0)), pl.BlockSpec((TM, LANE), lambda t: (t, 0))],
        out_shape=[jax.ShapeDtypeStruct((m, d), F32), jax.ShapeDtypeStruct((m, LANE), F32)],
        compiler_params=_cparams(),
        name="router",
    )(x1, modrows, wr, br)


def _row_copy(src_hbm, dst, sem, row, j):
    return pltpu.make_async_copy(src_hbm.at[pl.ds(row, 1), :], dst.at[pl.ds(j, 1), :], sem)


def _start_gather(src_hbm, idx_ref, buf, sem, slot, n_rows):
    def body(j, carry):
        _row_copy(src_hbm, buf.at[slot], sem.at[slot], idx_ref[0, 0, j], j).start()
        return carry
    lax.fori_loop(0, n_rows, body, 0, unroll=8)


def _wait_gather(src_hbm, buf, sem, slot, n_rows):
    def body(j, carry):
        _row_copy(src_hbm, buf.at[slot], sem.at[slot], 0, j).wait()
        return carry
    lax.fori_loop(0, n_rows, body, 0, unroll=8)


def _pipelined_gather(src_hbm, idx_ref, idx_next_ref, buf, sem, n_rows):
    i = pl.program_id(0)
    n = pl.num_programs(0)
    slot = i % 2

    @pl.when(i == 0)
    def _():
        _start_gather(src_hbm, idx_ref, buf, sem, 0, n_rows)

    @pl.when(i + 1 < n)
    def _():
        _start_gather(src_hbm, idx_next_ref, buf, sem, 1 - slot, n_rows)

    _wait_gather(src_hbm, buf, sem, slot, n_rows)
    return slot


def _expert_kernel(blk_exp_ref, tok_ref, tok_next_ref, h_hbm, w13_ref, w2_ref, o_ref, buf, sem):
    slot = _pipelined_gather(h_hbm, tok_ref, tok_next_ref, buf, sem, MOE_ROWS)
    x = buf[slot].astype(BF16)
    y = jnp.dot(x, w13_ref[0], preferred_element_type=F32)
    act = (_silu(y[:, :D_EXPERT]) * y[:, D_EXPERT:]).astype(BF16)
    o_ref[...] = jnp.dot(act, w2_ref[0], preferred_element_type=F32)


def _expert_call(blk_exp, tok_buf, h2, w13, w2):
    n_blk = blk_exp.shape[0]
    d = h2.shape[1]
    tok3 = tok_buf.reshape(n_blk, 1, MOE_ROWS)
    grid_spec = pltpu.PrefetchScalarGridSpec(
        num_scalar_prefetch=1,
        grid=(n_blk,),
        in_specs=[pl.BlockSpec((1, 1, MOE_ROWS), lambda i, be: (i, 0, 0), memory_space=pltpu.SMEM),
                  pl.BlockSpec((1, 1, MOE_ROWS), lambda i, be: (jnp.minimum(i + 1, n_blk - 1), 0, 0),
                               memory_space=pltpu.SMEM),
                  pl.BlockSpec(memory_space=pl.ANY),
                  pl.BlockSpec((1, d, 2 * D_EXPERT), lambda i, be: (be[i], 0, 0)),
                  pl.BlockSpec((1, D_EXPERT, d), lambda i, be: (be[i], 0, 0))],
        out_specs=pl.BlockSpec((MOE_ROWS, d), lambda i, be: (i, 0)),
        scratch_shapes=[pltpu.VMEM((2, MOE_ROWS, d), F32), pltpu.SemaphoreType.DMA((2,))],
    )
    return pl.pallas_call(
        _expert_kernel,
        grid_spec=grid_spec,
        out_shape=jax.ShapeDtypeStruct((n_blk * MOE_ROWS, d), F32),
        compiler_params=_cparams(),
        name="expert_ffn",
    )(blk_exp, tok3, tok3, h2, w13, w2)


def _combine_kernel(pos_ref, pos_next_ref, y_hbm, x_ref, r_ref, mod_ref, par_ref, o_ref, buf, sem, *, alpha):
    slot = _pipelined_gather(y_hbm, pos_ref, pos_next_ref, buf, sem, 2 * TM)
    r = r_ref[...]
    y2 = r[:, 0:1] * buf[slot, 0:TM, :] + r[:, 1:2] * buf[slot, TM:2 * TM, :]
    g2 = mod_ref[0, 5:6, :]
    o_ref[...] = _layernorm(alpha * x_ref[...] + g2 * y2, par_ref[0:1, :], par_ref[1:2, :])


def _combine_call(pos, yb, x1, route, modrows, par, alpha, tpb, n_ctx_tiles):
    m, d = x1.shape
    nt = m // TM
    n_batch = modrows.shape[0] - 1
    pos3 = pos.reshape(nt, 1, 2 * TM)

    def mod_map(t):
        return (jnp.where(t % tpb < n_ctx_tiles, n_batch, t // tpb), 0, 0)

    return pl.pallas_call(
        functools.partial(_combine_kernel, alpha=alpha),
        grid=(nt,),
        in_specs=[pl.BlockSpec((1, 1, 2 * TM), lambda t: (t, 0, 0), memory_space=pltpu.SMEM),
                  pl.BlockSpec((1, 1, 2 * TM), lambda t: (jnp.minimum(t + 1, nt - 1), 0, 0), memory_space=pltpu.SMEM),
                  pl.BlockSpec(memory_space=pl.ANY),
                  pl.BlockSpec((TM, d), lambda t: (t, 0)), pl.BlockSpec((TM, LANE), lambda t: (t, 0)),
                  pl.BlockSpec((1, 6, d), mod_map), _const_spec(par.shape)],
        out_specs=pl.BlockSpec((TM, d), lambda t: (t, 0)),
        out_shape=jax.ShapeDtypeStruct((m, d), F32),
        scratch_shapes=[pltpu.VMEM((2, 2 * TM, d), F32), pltpu.SemaphoreType.DMA((2,))],
        compiler_params=_cparams(),
        name="moe_combine",
    )(pos3, pos3, yb, x1, route, modrows, par)


def _rope_rot_perm():
    nf = QK_ROPE // 4
    src = np.zeros((QK_ROPE,), np.int32)
    sign = np.zeros((QK_ROPE,), np.float32)
    for a in range(2):
        for f in range(nf):
            i0 = a * 2 * nf + f
            i1 = a * 2 * nf + nf + f
            src[i0], sign[i0] = i1, -1.0
            src[i1], sign[i1] = i0, 1.0
    return src, sign


def _rope_tables(n_lat, n_ctx):
    nf = QK_ROPE // 4
    rows = n_lat // GRID_W
    inv = ROPE_BASE ** (-jnp.arange(nf, dtype=F32) / nf)
    r = jnp.repeat(jnp.arange(rows, dtype=F32), GRID_W)
    col = jnp.tile(jnp.arange(GRID_W, dtype=F32), rows)
    ang = jnp.stack([r[:, None] * inv, col[:, None] * inv], axis=1)
    cos = jnp.broadcast_to(jnp.cos(ang)[:, :, None, :], (n_lat, 2, 2, nf)).reshape(n_lat, QK_ROPE)
    sin = jnp.broadcast_to(jnp.sin(ang)[:, :, None, :], (n_lat, 2, 2, nf)).reshape(n_lat, QK_ROPE)
    cos = jnp.concatenate([jnp.ones((n_ctx, QK_ROPE), F32), cos], 0)
    sin = jnp.concatenate([jnp.zeros((n_ctx, QK_ROPE), F32), sin], 0)
    s = n_lat + n_ctx
    z = lambda w: jnp.zeros((s, w), F32)
    pad = LANE - QK_NOPE - QK_ROPE
    tqc = jnp.concatenate([jnp.full((s, QK_NOPE), MLA_SCALE, F32), cos * MLA_SCALE, z(pad)], 1)
    tqs = jnp.concatenate([z(QK_NOPE), sin * MLA_SCALE, z(pad)], 1)
    tkc = jnp.concatenate([z(QK_NOPE), cos, z(pad)], 1)
    tks = jnp.concatenate([z(QK_NOPE), sin, z(pad)], 1)
    return tqc, tqs, tkc, tks


def _head_pad_cols(w, width):
    k = w.shape[0]
    w = w.reshape(k, MLA_HEADS, width)
    return jnp.pad(w, ((0, 0), (0, 0), (0, HEAD_PAD - width))).reshape(k, MLA_HEADS * HEAD_PAD)


def _layer_weights(l, w_in, b_in, conv_a_w, conv_a_b, ln_a_g, ln_a_b, w_a_out, g_q, g_kv, w_uq, w_qr, w_uk, w_uv,
                   w_b_out, conv_c_w, a_log, dt_bias, g_o, w_c_out, w_out, ln1_g, ln1_b, w_rg, b_rg, w_re, b_re,
                   w1, w3, w2, ln2_g, ln2_b):
    src, sign = _rope_rot_perm()
    wi, bi = w_in[l], b_in[l][None, :]
    sizes = (2 * CONV_CH, Q_RANK, KV_RANK, QK_ROPE, 512, 512, 512, 512, 16, 16, 3 * D_MODEL)
    offs = np.concatenate([[0], np.cumsum(sizes)])

    def cols(a, i, j=None):
        return a[:, offs[i]:offs[(i if j is None else j) + 1]]

    def regroup(a):
        kr = cols(a, 3)
        z = lambda w: jnp.zeros((a.shape[0], w), a.dtype)
        kr_grp = jnp.concatenate([z(QK_NOPE), kr, z(32), z(QK_NOPE), kr[:, src] * sign, z(32)], 1)
        ba_grp = jnp.concatenate([cols(a, 8, 9), z(LANE - 32)], 1)
        return jnp.concatenate([cols(a, 0), cols(a, 1), cols(a, 2), kr_grp, cols(a, 4, 6), cols(a, 7), ba_grp,
                                cols(a, 10)], 1)

    w_all = regroup(wi).astype(BF16)
    b_all = regroup(bi)
    zl = jnp.zeros((LANE,), F32)
    gpar = jnp.stack([zl.at[16:32].set(a_log[l].reshape(-1)), zl.at[16:32].set(dt_bias[l].reshape(-1))], 0)
    gpar = jnp.concatenate([gpar, jnp.zeros((6, LANE), F32)], 0)

    wq = jnp.concatenate([w_uq[l].reshape(Q_RANK, MLA_HEADS, QK_NOPE), w_qr[l].reshape(Q_RANK, MLA_HEADS, QK_ROPE)], 2)
    wq = _head_pad_cols(wq.reshape(Q_RANK, -1), QK_NOPE + QK_ROPE).astype(BF16)
    wqr_rot = (w_qr[l].reshape(Q_RANK, MLA_HEADS, QK_ROPE)[:, :, src] * sign)
    wqr = jnp.concatenate([jnp.zeros((Q_RANK, MLA_HEADS, QK_NOPE), F32), wqr_rot], 2)
    wqr = _head_pad_cols(wqr.reshape(Q_RANK, -1), QK_NOPE + QK_ROPE).astype(BF16)
    wk = _head_pad_cols(w_uk[l], QK_NOPE).astype(BF16)
    wv = _head_pad_cols(w_uv[l], V_DIM).astype(BF16)
    vone = jnp.zeros((MLA_HEADS, HEAD_PAD), F32).at[:, V_DIM].set(1.0).reshape(1, -1)
    wb = jnp.pad(w_b_out[l].reshape(MLA_HEADS, V_DIM, D_MODEL), ((0, 0), (0, HEAD_PAD - V_DIM), (0, 0)))
    wb = wb.reshape(MLA_HEADS * HEAD_PAD, D_MODEL).astype(BF16)

    wr = jnp.concatenate([w_rg[l], w_re[l], jnp.zeros((D_MODEL, LANE - N_GROUPS - N_EXPERTS), F32)], 1)
    br = jnp.concatenate([b_rg[l], b_re[l], jnp.zeros((LANE - N_GROUPS - N_EXPERTS,), F32)])[None, :]
    return dict(
        w_all=w_all, b_all=b_all, gpar=gpar,
        conv_a_w=conv_a_w[l], conv_a_par=jnp.stack([conv_a_b[l], ln_a_g[l], ln_a_b[l]] + [jnp.zeros_like(ln_a_b[l])] * 5, 0),
        gq=g_q[l][None, :], gkv=g_kv[l][None, :], wq=wq, wqr=wqr, wk=wk, wv=wv, vone=vone,
        conv_c_w=conv_c_w[l],
        wa=w_a_out[l].astype(BF16), wb=wb, wc=w_c_out[l].astype(BF16), wo=w_out[l].astype(BF16),
        go=jnp.tile(g_o[l], GDN_HEADS)[None, :],
        ln1=jnp.stack([ln1_g[l], ln1_b[l]] + [jnp.zeros_like(ln1_g[l])] * 6, 0),
        ln2=jnp.stack([ln2_g[l], ln2_b[l]] + [jnp.zeros_like(ln2_g[l])] * 6, 0),
        wr=wr, br=br,
        w13=jnp.concatenate([w1[l], w3[l]], -1).astype(BF16), w2=w2[l].astype(BF16),
    )


def _routing_tables(route, n_tok):
    eid = route[:, 2:4].astype(jnp.int32).reshape(-1)
    n_as = 2 * n_tok
    n_blk = n_as // MOE_ROWS + N_EXPERTS
    order = jnp.argsort(eid)
    es = eid[order]
    sizes = jnp.bincount(eid, length=N_EXPERTS).astype(jnp.int32)
    psizes = (sizes + MOE_ROWS - 1) // MOE_ROWS * MOE_ROWS
    starts = jnp.cumsum(sizes) - sizes
    pends = jnp.cumsum(psizes)
    pstarts = pends - psizes
    blk_exp = jnp.minimum(jnp.searchsorted(pends, jnp.arange(n_blk, dtype=jnp.int32) * MOE_ROWS, side="right"),
                          N_EXPERTS - 1).astype(jnp.int32)
    prow = jnp.arange(n_blk * MOE_ROWS, dtype=jnp.int32)
    pe = jnp.repeat(blk_exp, MOE_ROWS)
    j = prow - pstarts[pe]
    valid = jnp.logical_and(j >= 0, j < sizes[pe])
    src_sorted = jnp.clip(starts[pe] + j, 0, n_as - 1)
    tok_buf = jnp.where(valid, order[src_sorted] // 2, 0).astype(jnp.int32)
    dest_sorted = pstarts[es] + jnp.arange(n_as, dtype=jnp.int32) - starts[es]
    inv = jnp.argsort(order)
    pos = dest_sorted[inv].reshape(n_tok, 2)
    return blk_exp, tok_buf, pos


def kernel(x, c, ctx, c_ctx, w_mod, b_mod, w_in, b_in, conv_a_w, conv_a_b, ln_a_g, ln_a_b, w_a_out, g_q, g_kv, w_uq,
           w_qr, w_uk, w_uv, w_b_out, conv_c_w, a_log, dt_bias, g_o, w_c_out, w_out, ln1_g, ln1_b, w_rg, b_rg, w_re,
           b_re, w1, w3, w2, ln2_g, ln2_b):
    n_batch, n_lat, d = x.shape
    n_ctx = ctx.shape[1]
    depth = w_mod.shape[0]
    alpha = (2 * depth) ** 0.25
    s_len = n_ctx + n_lat
    m = n_batch * s_len
    tpb = s_len // TM
    n_ctx_tiles = n_ctx // TM
    assert n_ctx % TM == 0 and n_lat % TM == 0 and n_ctx % CHUNK == 0 and s_len % n_ctx == 0

    xcat = jnp.concatenate([ctx, x], axis=1).reshape(m, d)
    cvec = jnp.concatenate([c, c_ctx[None, :], jnp.zeros((16 - n_batch - 1, d), F32)], 0)
    tabs = _rope_tables(n_lat, n_ctx)
    ones_blk = jnp.kron(jnp.eye(GDN_HEADS, dtype=F32), jnp.ones((GDN_DK, GDN_DK), F32)).astype(BF16)
    params = (w_in, b_in, conv_a_w, conv_a_b, ln_a_g, ln_a_b, w_a_out, g_q, g_kv, w_uq, w_qr, w_uk, w_uv, w_b_out,
              conv_c_w, a_log, dt_bias, g_o, w_c_out, w_out, ln1_g, ln1_b, w_rg, b_rg, w_re, b_re, w1, w3, w2,
              ln2_g, ln2_b)
    tq = TM

    for l in range(depth):
        p = _layer_weights(l, *params)
        mod = _mod_call(cvec, w_mod[l].astype(BF16), b_mod[l][None, :])
        modrows = mod[:n_batch + 1].reshape(n_batch + 1, 6, d)

        u, qd, kvd, kr, gqkv, gg, ba, mg = _inproj_call(xcat, modrows, p["w_all"], p["b_all"], p["gpar"], tpb,
                                                        n_ctx_tiles)
        za = _conv_a_call(u, p["conv_a_w"], p["conv_a_par"], tpb, n_ctx_tiles)
        qp, kp, vp = _mla_proj_call(qd, kvd, kr, tabs, p["gq"], p["gkv"], p["wq"], p["wqr"], p["wk"], p["wv"],
                                    p["vone"], tpb)
        zb_l = _attn_call(qp, kp, vp, n_batch, s_len, n_ctx, n_lat, s_len, tq)
        zb_c = _attn_call(qp, kp, vp, n_batch, s_len, 0, n_ctx, n_ctx, n_ctx)
        hp = MLA_HEADS * HEAD_PAD
        zb = jnp.concatenate([zb_c.reshape(n_batch, n_ctx, hp), zb_l.reshape(n_batch, n_lat, hp)], 1).reshape(m, hp)
        qkvn = _gdn_prep_call(gqkv, p["conv_c_w"], ones_blk, tpb, n_ctx_tiles)
        bgt2 = ba[:, :32].reshape(m // CHUNK, CHUNK, 2, 2, N_PAIRS, 2).transpose(0, 2, 3, 4, 5, 1)
        bgt2 = bgt2.reshape(m // CHUNK, 16, LANE)
        gw, gu0, gkd, gqg, gpm, gcx = _gdn_local_call(qkvn, ba, bgt2)
        of, ob = _gdn_scan_call(gw, gu0, gkd, gqg, gpm, gcx, n_batch, tpb, n_ctx_tiles)
        x1 = _merge_call(za, zb, of, ob, gg, mg, xcat, modrows, p["wa"], p["wb"], p["wc"], p["wo"], ones_blk, p["ln1"],
                         p["go"], alpha, tpb, n_ctx_tiles)
        h2, route = _router_call(x1, modrows, p["wr"], p["br"], tpb, n_ctx_tiles)
        blk_exp, tok_buf, pos = _routing_tables(route, m)
        yb = _expert_call(blk_exp, tok_buf, h2, p["w13"], p["w2"])
        pos_t = pos.reshape(m // TM, TM, 2).transpose(0, 2, 1).reshape(-1)
        xcat = _combine_call(pos_t, yb, x1, route, modrows, p["ln2"], alpha, tpb, n_ctx_tiles)

    return xcat.reshape(n_batch, s_len, d)[:, n_ctx:, :]
```

```python
import functools

import jax
import jax.numpy as jnp
import numpy as np
from jax import lax
from jax.experimental import pallas as pl
from jax.experimental.pallas import tpu as pltpu

F32 = jnp.float32
BF16 = jnp.bfloat16

D_MODEL = 1024
GRID_W = 64
EPS = 1e-6
CONV_CH = 512
CONV_K = 31
MLA_HEADS = 8
QK_NOPE = 64
QK_ROPE = 32
V_DIM = 64
Q_RANK = 384
KV_RANK = 256
ROPE_BASE = 10000.0
MLA_SCALE = (QK_NOPE + QK_ROPE) ** -0.5
GDN_HEADS = 8
GDN_DK = 64
GDN_DV = 64
SHORT_K = 5
CHUNK = 64
N_GROUPS = 4
EXP_PER_GROUP = 8
N_EXPERTS = N_GROUPS * EXP_PER_GROUP
D_EXPERT = 256

LANE = 128
TM = 256
HEAD_PAD = LANE
MOE_ROWS = 256
VMEM_LIMIT = 56 * 1024 * 1024

OFF_GLU = 0
OFF_QD = OFF_GLU + 2 * CONV_CH
OFF_KVD = OFF_QD + Q_RANK
OFF_KR = OFF_KVD + KV_RANK
OFF_GQKV = OFF_KR + 2 * LANE
OFF_GG = OFF_GQKV + 3 * GDN_HEADS * GDN_DK
OFF_BA = OFF_GG + GDN_HEADS * GDN_DV
OFF_MG = OFF_BA + LANE
P_ALL = OFF_MG + 3 * D_MODEL


def _cparams(n_grid=1):
    return pltpu.CompilerParams(dimension_semantics=("arbitrary",) * n_grid, vmem_limit_bytes=VMEM_LIMIT)


def _const_spec(shape):
    nd = len(shape)
    return pl.BlockSpec(shape, lambda *_: (0,) * nd, pipeline_mode=pl.Buffered(1))


def _bdot(a, b):
    return jnp.dot(a.astype(BF16), b.astype(BF16), preferred_element_type=F32)


def _hdot(a, b):
    return jnp.dot(a, b, precision=lax.Precision.HIGHEST, preferred_element_type=F32)


def _dot_nt(a, b):
    return lax.dot_general(a.astype(BF16), b.astype(BF16), (((1,), (1,)), ((), ())), preferred_element_type=F32)


def _dot_tn(a, b):
    return lax.dot_general(a.astype(BF16), b.astype(BF16), (((0,), (0,)), ((), ())), preferred_element_type=F32)


def _silu(x):
    return x * jax.nn.sigmoid(x)


def _group_sumsq(x, ones_blk):
    x2 = x * x
    hi = x2.astype(BF16)
    lo = (x2 - hi.astype(F32)).astype(BF16)
    return jnp.dot(hi, ones_blk, preferred_element_type=F32) + jnp.dot(lo, ones_blk, preferred_element_type=F32)


def _layernorm(x, g, b):
    mu = jnp.mean(x, -1, keepdims=True)
    xc = x - mu
    var = jnp.mean(xc * xc, -1, keepdims=True)
    return xc * lax.rsqrt(var + EPS) * g + b


def _mod_kernel(c_ref, w_ref, b_ref, o_ref):
    o_ref[...] = _bdot(_silu(c_ref[...]), w_ref[...]) + b_ref[...]


def _mod_call(cvec, w, b):
    rows, d = cvec.shape
    n = w.shape[1]
    tn = 1536
    return pl.pallas_call(
        _mod_kernel,
        grid=(n // tn,),
        in_specs=[pl.BlockSpec((rows, d), lambda j: (0, 0)),
                  pl.BlockSpec((d, tn), lambda j: (0, j)),
                  pl.BlockSpec((1, tn), lambda j: (0, j))],
        out_specs=pl.BlockSpec((rows, tn), lambda j: (0, j)),
        out_shape=jax.ShapeDtypeStruct((rows, n), F32),
        compiler_params=_cparams(),
        name="mod_vectors",
    )(cvec, w, b)


def _inproj_kernel(x_ref, mod_ref, w_ref, b_ref, gpar_ref,
                   u_ref, qd_ref, kvd_ref, kr_ref, gqkv_ref, gg_ref, ba_ref, mg_ref):
    x = x_ref[...]
    sh = mod_ref[0, 0:1, :]
    sc = mod_ref[0, 1:2, :]
    h = (x * (1.0 + sc) + sh).astype(BF16)

    def proj(off, n):
        return jnp.dot(h, w_ref[:, off:off + n], preferred_element_type=F32) + b_ref[:, off:off + n]

    y = proj(OFF_GLU, 2 * CONV_CH)
    u_ref[...] = y[:, :CONV_CH] * jax.nn.sigmoid(y[:, CONV_CH:])
    qd_ref[...] = proj(OFF_QD, Q_RANK)
    kvd_ref[...] = proj(OFF_KVD, KV_RANK)
    kr_ref[...] = proj(OFF_KR, 2 * LANE)
    gqkv_ref[...] = proj(OFF_GQKV, 3 * GDN_HEADS * GDN_DK)
    gg_ref[...] = proj(OFF_GG, GDN_HEADS * GDN_DV)
    y = proj(OFF_BA, LANE)
    lane = lax.broadcasted_iota(jnp.int32, y.shape, 1)
    z = y + gpar_ref[1:2, :]
    sp = jnp.maximum(z, 0.0) + jnp.log1p(jnp.exp(-jnp.abs(z)))
    ba_ref[...] = jnp.where(lane < 2 * GDN_HEADS, jax.nn.sigmoid(y), -jnp.exp(gpar_ref[0:1, :]) * sp)
    mg_ref[...] = jax.nn.sigmoid(proj(OFF_MG, 3 * D_MODEL)).astype(BF16)


def _inproj_call(xcat, modrows, w_all, b_all, gpar, tpb, n_ctx_tiles):
    m, d = xcat.shape
    nt = m // TM
    n_batch = modrows.shape[0] - 1

    def mod_map(t):
        return (jnp.where(t % tpb < n_ctx_tiles, n_batch, t // tpb), 0, 0)

    widths = (CONV_CH, Q_RANK, KV_RANK, 2 * LANE, 3 * GDN_HEADS * GDN_DK, GDN_HEADS * GDN_DV, LANE, 3 * D_MODEL)
    dtypes = (F32, F32, F32, F32, F32, F32, F32, BF16)
    return pl.pallas_call(
        _inproj_kernel,
        grid=(nt,),
        in_specs=[pl.BlockSpec((TM, d), lambda t: (t, 0)),
                  pl.BlockSpec((1, 6, d), mod_map),
                  _const_spec(w_all.shape), _const_spec(b_all.shape), _const_spec(gpar.shape)],
        out_specs=[pl.BlockSpec((TM, w), lambda t: (t, 0)) for w in widths],
        out_shape=[jax.ShapeDtypeStruct((m, w), dt) for w, dt in zip(widths, dtypes)],
        compiler_params=_cparams(),
        name="in_proj",
    )(xcat, modrows, w_all, b_all, gpar)


CONV_HALO = 16


def _conv_a_kernel(top_ref, cur_ref, bot_ref, w_ref, par_ref, o_ref, win_ref, *, tpb, n_ctx_tiles):
    t = pl.program_id(0)
    r = t % tpb
    top_ok = jnp.logical_and(r != 0, r != n_ctx_tiles)
    bot_ok = jnp.logical_and(r != n_ctx_tiles - 1, r != tpb - 1)
    win_ref[0:CONV_HALO, :] = jnp.where(top_ok, top_ref[...], 0.0)
    win_ref[CONV_HALO:CONV_HALO + TM, :] = cur_ref[...]
    win_ref[CONV_HALO + TM:, :] = jnp.where(bot_ok, bot_ref[...], 0.0)
    acc = jnp.zeros((TM, CONV_CH), F32)
    base = CONV_HALO - CONV_K // 2
    for k in range(CONV_K):
        acc = acc + win_ref[pl.ds(base + k, TM), :] * w_ref[k:k + 1, :]
    y = _layernorm(acc + par_ref[0:1, :], par_ref[1:2, :], par_ref[2:3, :])
    o_ref[...] = _silu(y).astype(BF16)


def _conv_a_call(u, w, par, tpb, n_ctx_tiles):
    m, c = u.shape
    nt = m // TM
    hb = TM // CONV_HALO
    last = m // CONV_HALO - 1
    wp = jnp.zeros((32, c), F32).at[:CONV_K].set(w)
    return pl.pallas_call(
        functools.partial(_conv_a_kernel, tpb=tpb, n_ctx_tiles=n_ctx_tiles),
        grid=(nt,),
        in_specs=[pl.BlockSpec((CONV_HALO, c), lambda t: (jnp.maximum(t * hb - 1, 0), 0)),
                  pl.BlockSpec((TM, c), lambda t: (t, 0)),
                  pl.BlockSpec((CONV_HALO, c), lambda t: (jnp.minimum((t + 1) * hb, last), 0)),
                  _const_spec(wp.shape), _const_spec(par.shape)],
        out_specs=pl.BlockSpec((TM, c), lambda t: (t, 0)),
        out_shape=jax.ShapeDtypeStruct((m, c), BF16),
        scratch_shapes=[pltpu.VMEM((TM + 2 * CONV_HALO, c), F32)],
        compiler_params=_cparams(),
        name="conv_branch",
    )(u, u, u, wp, par)


def _mla_proj_kernel(qd_ref, kvd_ref, kr_ref, tqc_ref, tqs_ref, tkc_ref, tks_ref, gq_ref, gkv_ref,
                     wq_ref, wqr_ref, wk_ref, wv_ref, vone_ref, q_ref, k_ref, v_ref):
    qd = qd_ref[...]
    cq = qd * lax.rsqrt(jnp.mean(qd * qd, -1, keepdims=True) + EPS) * gq_ref[...]
    qf = _bdot(cq, wq_ref[...])
    qr = _bdot(cq, wqr_ref[...])
    kvd = kvd_ref[...]
    ckv = kvd * lax.rsqrt(jnp.mean(kvd * kvd, -1, keepdims=True) + EPS) * gkv_ref[...]
    kn = _bdot(ckv, wk_ref[...])
    krf = kr_ref[:, :LANE] * tkc_ref[...] + kr_ref[:, LANE:] * tks_ref[...]
    tqc = tqc_ref[...]
    tqs = tqs_ref[...]
    for h in range(MLA_HEADS):
        sl = slice(h * HEAD_PAD, (h + 1) * HEAD_PAD)
        q_ref[:, sl] = (qf[:, sl] * tqc + qr[:, sl] * tqs).astype(BF16)
        k_ref[:, sl] = (kn[:, sl] + krf).astype(BF16)
    v_ref[...] = (_bdot(ckv, wv_ref[...]) + vone_ref[...]).astype(BF16)


def _mla_proj_call(qd, kvd, kr, tabs, gq, gkv, wq, wqr, wk, wv, vone, tpb):
    m = qd.shape[0]
    nt = m // TM
    hp = MLA_HEADS * HEAD_PAD
    tab_spec = pl.BlockSpec((TM, LANE), lambda t: (t % tpb, 0))
    row = lambda w: pl.BlockSpec((TM, w), lambda t: (t, 0))
    return pl.pallas_call(
        _mla_proj_kernel,
        grid=(nt,),
        in_specs=[row(Q_RANK), row(KV_RANK), row(2 * LANE), tab_spec, tab_spec, tab_spec, tab_spec,
                  _const_spec(gq.shape), _const_spec(gkv.shape), _const_spec(wq.shape), _const_spec(wqr.shape),
                  _const_spec(wk.shape), _const_spec(wv.shape), _const_spec(vone.shape)],
        out_specs=[row(hp), row(hp), row(hp)],
        out_shape=[jax.ShapeDtypeStruct((m, hp), BF16)] * 3,
        compiler_params=_cparams(),
        name="mla_proj",
    )(qd, kvd, kr, *tabs, gq, gkv, wq, wqr, wk, wv, vone)


def _attn_kernel(q_ref, k_ref, v_ref, o_ref):
    for h in range(MLA_HEADS):
        sl = slice(h * HEAD_PAD, (h + 1) * HEAD_PAD)
        s = lax.dot_general(q_ref[:, sl], k_ref[:, sl], (((1,), (1,)), ((), ())), preferred_element_type=F32)
        m = jnp.max(s, axis=-1, keepdims=True)
        p = jnp.exp(s - m).astype(BF16)
        o = jnp.dot(p, v_ref[:, sl], preferred_element_type=F32)
        denom = o[:, V_DIM:V_DIM + 1]
        o_ref[:, sl] = (o * (1.0 / denom)).astype(BF16)


def _attn_call(q, k, v, n_batch, s_len, q_start, q_len, k_len, tq):
    hp = MLA_HEADS * HEAD_PAD
    nq = q_len // tq
    q0 = q_start // tq
    spb = s_len // tq
    kb = s_len // k_len
    return pl.pallas_call(
        _attn_kernel,
        grid=(n_batch, nq),
        in_specs=[pl.BlockSpec((tq, hp), lambda b, i: (b * spb + q0 + i, 0)),
                  pl.BlockSpec((k_len, hp), lambda b, i: (b * kb, 0), pipeline_mode=pl.Buffered(1)),
                  pl.BlockSpec((k_len, hp), lambda b, i: (b * kb, 0), pipeline_mode=pl.Buffered(1))],
        out_specs=pl.BlockSpec((tq, hp), lambda b, i: (b * nq + i, 0)),
        out_shape=jax.ShapeDtypeStruct((n_batch * q_len, hp), BF16),
        compiler_params=_cparams(2),
        name="attention",
    )(q, k, v)


GDN_HALO = 8


def _gdn_prep_kernel(top_ref, cur_ref, bot_ref, w_ref, ones_ref, o_ref, win_ref, *, tpb, n_ctx_tiles):
    t = pl.program_id(0)
    r = t % tpb
    top_ok = jnp.logical_and(r != 0, r != n_ctx_tiles)
    bot_ok = jnp.logical_and(r != n_ctx_tiles - 1, r != tpb - 1)
    win_ref[0:GDN_HALO, :] = jnp.where(top_ok, top_ref[...], 0.0)
    win_ref[GDN_HALO:GDN_HALO + TM, :] = cur_ref[...]
    win_ref[GDN_HALO + TM:, :] = jnp.where(bot_ok, bot_ref[...], 0.0)
    c = 3 * GDN_HEADS * GDN_DK
    acc = jnp.zeros((TM, c), F32)
    base = GDN_HALO - SHORT_K // 2
    for k in range(SHORT_K):
        acc = acc + win_ref[pl.ds(base + k, TM), :] * w_ref[k:k + 1, :]
    u = _silu(acc)
    qk = GDN_HEADS * GDN_DK
    q = u[:, :qk]
    kk = u[:, qk:2 * qk]
    ones_blk = ones_ref[...]
    q = q * lax.rsqrt(_group_sumsq(q, ones_blk) + EPS) * (GDN_DK ** -0.5)
    kk = kk * lax.rsqrt(_group_sumsq(kk, ones_blk) + EPS)
    o_ref[:, :qk] = q.astype(BF16)
    o_ref[:, qk:2 * qk] = kk.astype(BF16)
    o_ref[:, 2 * qk:] = u[:, 2 * qk:].astype(BF16)


def _gdn_prep_call(gqkv, w, ones_blk, tpb, n_ctx_tiles):
    m, c = gqkv.shape
    nt = m // TM
    hb = TM // GDN_HALO
    last = m // GDN_HALO - 1
    wp = jnp.zeros((8, c), F32).at[:SHORT_K].set(w)
    return pl.pallas_call(
        functools.partial(_gdn_prep_kernel, tpb=tpb, n_ctx_tiles=n_ctx_tiles),
        grid=(nt,),
        in_specs=[pl.BlockSpec((GDN_HALO, c), lambda t: (jnp.maximum(t * hb - 1, 0), 0)),
                  pl.BlockSpec((TM, c), lambda t: (t, 0)),
                  pl.BlockSpec((GDN_HALO, c), lambda t: (jnp.minimum((t + 1) * hb, last), 0)),
                  _const_spec(wp.shape), _const_spec(ones_blk.shape)],
        out_specs=pl.BlockSpec((TM, c), lambda t: (t, 0)),
        out_shape=jax.ShapeDtypeStruct((m, c), BF16),
        scratch_shapes=[pltpu.VMEM((TM + 2 * GDN_HALO, c), F32)],
        compiler_params=_cparams(),
        name="gdn_prep",
    )(gqkv, gqkv, gqkv, wp, ones_blk)


N_PAIRS = GDN_HEADS // 2
CPT = TM // CHUNK


def _left_half(shape):
    return lax.broadcasted_iota(jnp.int32, shape, len(shape) - 1) < GDN_DK


def _blockdiag(x2):
    left = _left_half(x2.shape)
    zero = jnp.zeros_like(x2)
    return jnp.concatenate([jnp.where(left, x2, zero), jnp.where(left, zero, x2)], axis=0).astype(BF16)


def _lane_bcast_pair(x, c0):
    shape = (x.shape[0], LANE)
    return jnp.where(_left_half(shape), jnp.broadcast_to(x[:, c0:c0 + 1], shape),
                     jnp.broadcast_to(x[:, c0 + 1:c0 + 2], shape))


def _gdn_local_kernel(qkv_ref, bg_ref, bgt_ref, w_ref, u0_ref, kd_ref, qg_ref, pm_ref, gcx_ref):
    qk = GDN_HEADS * GDN_DK
    nh2 = 2 * GDN_HEADS
    ri = lax.broadcasted_iota(jnp.int32, (CHUNK, LANE), 0)
    ci = lax.broadcasted_iota(jnp.int32, (CHUNK, LANE), 1) % CHUNK
    incl = (ri >= ci, ri <= ci)
    strict = (ri > ci, ri < ci)
    eye2 = (ri == ci).astype(F32)
    r64 = lax.broadcasted_iota(jnp.int32, (CHUNK, CHUNK), 0)
    c64 = lax.broadcasted_iota(jnp.int32, (CHUNK, CHUNK), 1)
    m_incl = ((r64 >= c64).astype(F32), (r64 <= c64).astype(F32))
    rj = lax.broadcasted_iota(jnp.int32, (LANE, LANE), 0)
    cl = lax.broadcasted_iota(jnp.int32, (LANE, LANE), 1)
    same = (rj // CHUNK) == (cl // CHUNK)
    m_incl_t = (jnp.logical_and(same, rj % CHUNK <= cl % CHUNK).astype(F32),
                jnp.logical_and(same, rj % CHUNK >= cl % CHUNK).astype(F32))
    ones_bd = same.astype(F32)

    chains = []
    for j in range(CPT):
        rows = slice(j * CHUNK, (j + 1) * CHUNK)
        bg = bg_ref[rows, :]
        bgt = bgt_ref[j]
        gam = (_hdot(m_incl[0], bg), _hdot(m_incl[1], bg))
        gam_t = (_hdot(bgt, m_incl_t[0]), _hdot(bgt, m_incl_t[1]))
        gt_rows = _hdot(bgt, ones_bd)
        gcx_ref[j * 8:(j + 1) * 8, :] = jnp.exp(gt_rows[8:16])
        for p in range(N_PAIRS):
            ls = slice(p * LANE, (p + 1) * LANE)
            q2 = qkv_ref[rows, p * LANE:(p + 1) * LANE]
            k2 = qkv_ref[rows, qk + p * LANE:qk + (p + 1) * LANE]
            v2 = qkv_ref[rows, 2 * qk + p * LANE:2 * qk + (p + 1) * LANE]
            left = _left_half(k2.shape)
            zero = jnp.zeros_like(k2)
            k_bd = jnp.concatenate([jnp.where(left, k2, zero), jnp.where(left, zero, k2)], axis=0)
            kq = lax.dot_general(jnp.concatenate([k2, q2], axis=0), k_bd, (((1,), (1,)), ((), ())),
                                 preferred_element_type=F32)
            kk2, qk2 = kq[:CHUNK], kq[CHUNK:]
            k2f, q2f, v2f = k2.astype(F32), q2.astype(F32), v2.astype(F32)
            for d in range(2):
                c0 = d * GDN_HEADS + 2 * p
                r = 8 + d * N_PAIRS + p
                b2 = _lane_bcast_pair(bg, c0)
                gcol2 = _lane_bcast_pair(gam[d], nh2 + c0)
                grow2 = gam_t[d][r:r + 1, :]
                gt2 = gt_rows[r:r + 1, :]
                decay2 = jnp.where(incl[d], jnp.exp(jnp.where(incl[d], gcol2 - grow2, 0.0)), 0.0)
                n2 = -jnp.where(strict[d], b2 * decay2 * kk2, 0.0)
                eg2 = jnp.exp(gcol2)
                rhs = jnp.concatenate([_blockdiag((b2 * eg2) * k2f), _blockdiag(b2 * v2f)], axis=1)
                kd_ref[d, rows, ls] = (k2f * jnp.exp(gt2 - gcol2)).astype(BF16)
                qg_ref[d, rows, ls] = (q2f * eg2).astype(BF16)
                pm_ref[d, rows, ls] = (decay2 * qk2).astype(BF16)
                chains.append(dict(d=d, rows=rows, ls=ls, pw=n2, t=eye2 + n2, rhs=rhs))

    for ch in chains:
        ch["pw"] = jnp.dot(ch["pw"].astype(BF16), _blockdiag(ch["pw"]), preferred_element_type=F32)
    for _ in range(4):
        for ch in chains:
            res = jnp.dot(jnp.concatenate([ch["pw"], ch["t"]], axis=0).astype(BF16), _blockdiag(ch["pw"]),
                          preferred_element_type=F32)
            ch["pw"] = res[:CHUNK]
            ch["t"] = ch["t"] + res[CHUNK:]
    for ch in chains:
        ch["t"] = ch["t"] + jnp.dot(ch["t"].astype(BF16), _blockdiag(ch["pw"]), preferred_element_type=F32)
    for ch in chains:
        x = jnp.dot(ch["t"].astype(BF16), ch["rhs"], preferred_element_type=F32)
        w_ref[ch["d"], ch["rows"], ch["ls"]] = x[:, :LANE].astype(BF16)
        u0_ref[ch["d"], ch["rows"], ch["ls"]] = x[:, LANE:]


def _gdn_local_call(qkvn, bg, bgt2):
    m, c = qkvn.shape
    nt = m // TM
    dv = GDN_HEADS * GDN_DV
    spec2 = pl.BlockSpec((2, TM, dv), lambda t: (0, t, 0))
    shp = lambda dt: jax.ShapeDtypeStruct((2, m, dv), dt)
    return pl.pallas_call(
        _gdn_local_kernel,
        grid=(nt,),
        in_specs=[pl.BlockSpec((TM, c), lambda t: (t, 0)), pl.BlockSpec((TM, LANE), lambda t: (t, 0)),
                  pl.BlockSpec((CPT, 16, LANE), lambda t: (t, 0, 0))],
        out_specs=[spec2, spec2, spec2, spec2, spec2, pl.BlockSpec((CPT * 8, LANE), lambda t: (t, 0))],
        out_shape=[shp(BF16), shp(F32), shp(BF16), shp(BF16), shp(BF16),
                   jax.ShapeDtypeStruct((m // CHUNK * 8, LANE), F32)],
        compiler_params=_cparams(),
        name="gdn_local",
    )(qkvn, bg, bgt2)


def _gdn_scan_kernel(wf, u0f, kdf, qgf, pmf, gcf, wb, u0b, kdb, qgb, pmb, gcb, of_ref, ob_ref, s_ref):
    @pl.when(pl.program_id(1) == 0)
    def _():
        s_ref[...] = jnp.zeros_like(s_ref)

    refs = ((wf, u0f, kdf, qgf, pmf, gcf, of_ref), (wb, u0b, kdb, qgb, pmb, gcb, ob_ref))
    left = _left_half((CHUNK, LANE))
    state = [s_ref[i] for i in range(2 * N_PAIRS)]
    for sub in range(CPT):
        cur = []
        for d in range(2):
            j = sub if d == 0 else CPT - 1 - sub
            rows = slice(j * CHUNK, (j + 1) * CHUNK)
            for p in range(N_PAIRS):
                cur.append((d, p, rows, slice(p * LANE, (p + 1) * LANE), j * 8 + d * N_PAIRS + p))
        res = []
        for d, p, rows, ls, gr in cur:
            w_r, _, _, qg_r, _, _, _ = refs[d]
            lhs = jnp.concatenate([w_r[rows, ls], qg_r[rows, ls]], axis=0)
            res.append(jnp.dot(lhs, _blockdiag(state[d * N_PAIRS + p]), preferred_element_type=F32))
        us = []
        for (d, p, rows, ls, gr), r in zip(cur, res):
            us.append(refs[d][1][rows, ls] - r[:CHUNK])
        for (d, p, rows, ls, gr), r, u in zip(cur, res, us):
            _, _, kd_r, _, pm_r, gc_r, o_r = refs[d]
            o_r[rows, ls] = r[CHUNK:] + jnp.dot(pm_r[rows, ls], _blockdiag(u), preferred_element_type=F32)
            upd = lax.dot_general(kd_r[rows, ls], u.astype(BF16), (((0,), (0,)), ((), ())),
                                  preferred_element_type=F32)
            i = d * N_PAIRS + p
            state[i] = gc_r[gr:gr + 1, :] * state[i] + jnp.where(left, upd[:CHUNK], upd[CHUNK:])
    for i in range(2 * N_PAIRS):
        s_ref[i] = state[i]


def _gdn_scan_call(w, u0, kd, qg, pm, gcx, n_batch, tpb, n_ctx_tiles):
    m, dv = w.shape[1], w.shape[2]

    def fwd_tile(b, j):
        return b * tpb + j

    def bwd_tile(b, j):
        return b * tpb + jnp.where(j < n_ctx_tiles, n_ctx_tiles - 1 - j, tpb - 1 + n_ctx_tiles - j)

    def specs(d, tile):
        big = pl.BlockSpec((None, TM, dv), lambda b, j: (d, tile(b, j), 0))
        return [big, big, big, big, big, pl.BlockSpec((CPT * 8, LANE), lambda b, j: (tile(b, j), 0))]

    return pl.pallas_call(
        _gdn_scan_kernel,
        grid=(n_batch, tpb),
        in_specs=specs(0, fwd_tile) + specs(1, bwd_tile),
        out_specs=[pl.BlockSpec((TM, dv), lambda b, j: (fwd_tile(b, j), 0)),
                   pl.BlockSpec((TM, dv), lambda b, j: (bwd_tile(b, j), 0))],
        out_shape=[jax.ShapeDtypeStruct((m, dv), F32)] * 2,
        scratch_shapes=[pltpu.VMEM((2 * N_PAIRS, GDN_DK, LANE), F32)],
        compiler_params=_cparams(2),
        name="gdn_scan",
    )(w, u0, kd, qg, pm, gcx, w, u0, kd, qg, pm, gcx)


def _merge_kernel(za_ref, zb_ref, of_ref, ob_ref, gg_ref, mg_ref, x_ref, mod_ref, wa_ref, wb_ref, wc_ref, wo_ref,
                  ones_ref, par_ref, go_ref, o_ref, *, alpha):
    oc = of_ref[...] + ob_ref[...]
    ms = _group_sumsq(oc, ones_ref[...]) * (1.0 / GDN_DV)
    zc = oc * lax.rsqrt(ms + EPS) * go_ref[...] * _silu(gg_ref[...])
    ya = jnp.dot(za_ref[...], wa_ref[...], preferred_element_type=F32)
    yb = jnp.dot(zb_ref[...], wb_ref[...], preferred_element_type=F32)
    yc = _bdot(zc, wc_ref[...])
    d = D_MODEL
    mix = (mg_ref[:, 0:d].astype(F32) * ya + mg_ref[:, d:2 * d].astype(F32) * yb
           + mg_ref[:, 2 * d:3 * d].astype(F32) * yc)
    y = _bdot(mix, wo_ref[...])
    g1 = mod_ref[0, 2:3, :]
    o_ref[...] = _layernorm(alpha * x_ref[...] + g1 * y, par_ref[0:1, :], par_ref[1:2, :])


def _tile_maps(n_batch, tpb, n_ctx_tiles, lat_only):
    if lat_only:
        lpb = tpb - n_ctx_tiles
        return n_batch * lpb, (lambda t: (t // lpb) * tpb + n_ctx_tiles + t % lpb), (lambda t: t // lpb)
    return n_batch * tpb, (lambda t: t), (lambda t: jnp.where(t % tpb < n_ctx_tiles, n_batch, t // tpb))


def _merge_call(za, zb, of, ob, gg, mg, xcat, modrows, wa, wb, wc, wo, ones_blk, par, go, alpha, tpb, n_ctx_tiles,
                lat_only):
    d = xcat.shape[1]
    nt, src_tile, mod_row = _tile_maps(modrows.shape[0] - 1, tpb, n_ctx_tiles, lat_only)
    row = lambda w: pl.BlockSpec((TM, w), lambda t: (src_tile(t), 0))
    own = lambda w: pl.BlockSpec((TM, w), lambda t: (t, 0))
    return pl.pallas_call(
        functools.partial(_merge_kernel, alpha=alpha),
        grid=(nt,),
        in_specs=[row(za.shape[1]), own(zb.shape[1]), row(of.shape[1]), row(ob.shape[1]), row(gg.shape[1]),
                  row(mg.shape[1]), row(d),
                  pl.BlockSpec((1, 6, d), lambda t: (mod_row(t), 0, 0)),
                  _const_spec(wa.shape), _const_spec(wb.shape), _const_spec(wc.shape), _const_spec(wo.shape),
                  _const_spec(ones_blk.shape), _const_spec(par.shape), _const_spec(go.shape)],
        out_specs=own(d),
        out_shape=jax.ShapeDtypeStruct((nt * TM, d), F32),
        compiler_params=_cparams(),
        name="merge",
    )(za, zb, of, ob, gg, mg, xcat, modrows, wa, wb, wc, wo, ones_blk, par, go)


ROW_TILES = D_MODEL // LANE


def _to_token_tiles(x):
    chunks = jnp.stack([x[:, s * LANE:(s + 1) * LANE] for s in range(ROW_TILES)], axis=0)
    return pltpu.einshape("stl->tsl", chunks)


def _from_token_tiles(x3):
    chunks = pltpu.einshape("tsl->stl", x3)
    return [chunks[s] for s in range(ROW_TILES)]


def _router_kernel(x_ref, mod_ref, w_ref, b_ref, h_ref, r_ref, cnt_ref, cnt_scr):
    @pl.when(pl.program_id(0) == 0)
    def _():
        cnt_scr[...] = jnp.zeros_like(cnt_scr)

    sh = mod_ref[0, 3:4, :]
    sc = mod_ref[0, 4:5, :]
    h = x_ref[...] * (1.0 + sc) + sh
    h_ref[...] = _to_token_tiles(h)
    lg = _hdot(h, w_ref[...]) + b_ref[...]
    lane = lax.broadcasted_iota(jnp.int32, lg.shape, 1)
    neg = jnp.float32(-1e30)
    big = jnp.int32(1 << 20)
    is_g = lane < N_GROUPS
    gmax = jnp.max(jnp.where(is_g, lg, neg), axis=-1, keepdims=True)
    gsum = jnp.sum(jnp.where(is_g, jnp.exp(lg - gmax), 0.0), axis=-1, keepdims=True)
    gp = 1.0 / gsum
    gi = jnp.min(jnp.where(jnp.logical_and(is_g, lg == gmax), lane, big), axis=-1, keepdims=True)
    lo = N_GROUPS + gi * EXP_PER_GROUP
    in_grp = jnp.logical_and(lane >= lo, lane < lo + EXP_PER_GROUP)
    v1 = jnp.max(jnp.where(in_grp, lg, neg), axis=-1, keepdims=True)
    l1 = jnp.min(jnp.where(jnp.logical_and(in_grp, lg == v1), lane, big), axis=-1, keepdims=True)
    rest = jnp.logical_and(in_grp, lane != l1)
    v2 = jnp.max(jnp.where(rest, lg, neg), axis=-1, keepdims=True)
    l2 = jnp.min(jnp.where(jnp.logical_and(rest, lg == v2), lane, big), axis=-1, keepdims=True)
    e2 = jnp.exp(v2 - v1)
    p1 = 1.0 / (1.0 + e2)
    p2 = e2 / (1.0 + e2)
    out = jnp.where(lane == 0, gp * p1, 0.0)
    out = jnp.where(lane == 1, gp * p2, out)
    out = jnp.where(lane == 2, (l1 - N_GROUPS).astype(F32), out)
    out = jnp.where(lane == 3, (l2 - N_GROUPS).astype(F32), out)
    chosen = jnp.logical_or(lane == l1, lane == l2)
    onehot = jnp.where(chosen, 1.0, 0.0)
    ri = lax.broadcasted_iota(jnp.int32, (TM, TM), 0)
    ci = lax.broadcasted_iota(jnp.int32, (TM, TM), 1)
    earlier = jnp.dot(jnp.where(ri > ci, 1.0, 0.0).astype(BF16), onehot.astype(BF16), preferred_element_type=F32)
    before = earlier + cnt_scr[0:1, :]
    out = jnp.where(lane == 4, jnp.sum(jnp.where(lane == l1, before, 0.0), axis=-1, keepdims=True), out)
    out = jnp.where(lane == 5, jnp.sum(jnp.where(lane == l2, before, 0.0), axis=-1, keepdims=True), out)
    r_ref[...] = out
    cnt = cnt_scr[...] + jnp.sum(onehot, axis=0, keepdims=True)
    cnt_scr[...] = cnt
    cnt_ref[...] = cnt


def _router_call(x1, modrows, wr, br, mod_row):
    m, d = x1.shape
    nt = m // TM
    return pl.pallas_call(
        _router_kernel,
        grid=(nt,),
        in_specs=[pl.BlockSpec((TM, d), lambda t: (t, 0)), pl.BlockSpec((1, 6, d), lambda t: (mod_row(t), 0, 0)),
                  _const_spec(wr.shape), _const_spec(br.shape)],
        out_specs=[pl.BlockSpec((TM, ROW_TILES, LANE), lambda t: (t, 0, 0)), pl.BlockSpec((TM, LANE), lambda t: (t, 0)),
                   pl.BlockSpec((8, LANE), lambda t: (0, 0))],
        out_shape=[jax.ShapeDtypeStruct((m, ROW_TILES, LANE), F32), jax.ShapeDtypeStruct((m, LANE), F32),
                   jax.ShapeDtypeStruct((8, LANE), F32)],
        scratch_shapes=[pltpu.VMEM((8, LANE), F32)],
        compiler_params=_cparams(),
        name="router",
    )(x1, modrows, wr, br)


GATHER_UNROLL = 8


def _row_copy(src_hbm, dst, sem, row, j):
    return pltpu.make_async_copy(src_hbm.at[pl.ds(row, 1)], dst.at[pl.ds(j, 1)], sem)


def _start_gather(src_hbm, idx_ref, buf, sem, slot, n_rows):
    def body(jj, carry):
        for u in range(GATHER_UNROLL):
            j = jj * GATHER_UNROLL + u
            _row_copy(src_hbm, buf.at[slot], sem.at[slot], idx_ref[0, 0, j], j).start(priority=u % 2)
        return carry
    lax.fori_loop(0, n_rows // GATHER_UNROLL, body, 0)


def _wait_gather(src_hbm, buf, sem, slot, n_rows):
    def body(j, carry):
        _row_copy(src_hbm, buf.at[slot], sem.at[slot], 0, j).wait()
        return carry
    lax.fori_loop(0, n_rows, body, 0, unroll=8)


def _pipelined_gather(src_hbm, idx_ref, idx_next_ref, buf, sem, n_rows):
    i = pl.program_id(0)
    n = pl.num_programs(0)
    slot = i % 2

    @pl.when(i == 0)
    def _():
        _start_gather(src_hbm, idx_ref, buf, sem, 0, n_rows)

    @pl.when(i + 1 < n)
    def _():
        _start_gather(src_hbm, idx_next_ref, buf, sem, 1 - slot, n_rows)

    _wait_gather(src_hbm, buf, sem, slot, n_rows)
    return slot


def _expert_kernel(blk_exp_ref, tok_ref, tok_next_ref, h_hbm, w13_ref, w2_ref, o_ref, buf, sem):
    slot = _pipelined_gather(h_hbm, tok_ref, tok_next_ref, buf, sem, MOE_ROWS)
    x = jnp.concatenate(_from_token_tiles(buf[slot]), axis=-1).astype(BF16)
    y = jnp.dot(x, w13_ref[0], preferred_element_type=F32)
    act = (_silu(y[:, :D_EXPERT]) * y[:, D_EXPERT:]).astype(BF16)
    o_ref[...] = _to_token_tiles(jnp.dot(act, w2_ref[0], preferred_element_type=F32))


def _expert_call(blk_exp, tok_buf, h3, w13, w2):
    n_blk = blk_exp.shape[0]
    d = D_MODEL
    tok3 = tok_buf.reshape(n_blk, 1, MOE_ROWS)
    grid_spec = pltpu.PrefetchScalarGridSpec(
        num_scalar_prefetch=1,
        grid=(n_blk,),
        in_specs=[pl.BlockSpec((1, 1, MOE_ROWS), lambda i, be: (i, 0, 0), memory_space=pltpu.SMEM),
                  pl.BlockSpec((1, 1, MOE_ROWS), lambda i, be: (jnp.minimum(i + 1, n_blk - 1), 0, 0),
                               memory_space=pltpu.SMEM),
                  pl.BlockSpec(memory_space=pl.ANY),
                  pl.BlockSpec((1, d, 2 * D_EXPERT), lambda i, be: (be[i], 0, 0)),
                  pl.BlockSpec((1, D_EXPERT, d), lambda i, be: (be[i], 0, 0))],
        out_specs=pl.BlockSpec((MOE_ROWS, ROW_TILES, LANE), lambda i, be: (i, 0, 0)),
        scratch_shapes=[pltpu.VMEM((2, MOE_ROWS, ROW_TILES, LANE), F32), pltpu.SemaphoreType.DMA((2,))],
    )
    return pl.pallas_call(
        _expert_kernel,
        grid_spec=grid_spec,
        out_shape=jax.ShapeDtypeStruct((n_blk * MOE_ROWS, ROW_TILES, LANE), F32),
        compiler_params=_cparams(),
        name="expert_ffn",
    )(blk_exp, tok3, tok3, h3, w13, w2)


def _combine_kernel(pos_ref, pos_next_ref, y_hbm, x_ref, r_ref, mod_ref, par_ref, o_ref, buf, sem, *, alpha):
    slot = _pipelined_gather(y_hbm, pos_ref, pos_next_ref, buf, sem, 2 * TM)
    r = r_ref[...]
    g0 = jnp.broadcast_to(r[:, 0:1], (TM, LANE))
    g1 = jnp.broadcast_to(r[:, 1:2], (TM, LANE))
    y2 = jnp.concatenate([g0 * c[:TM] + g1 * c[TM:] for c in _from_token_tiles(buf[slot])], axis=-1)
    g2 = mod_ref[0, 5:6, :]
    o_ref[...] = _layernorm(alpha * x_ref[...] + g2 * y2, par_ref[0:1, :], par_ref[1:2, :])


def _combine_call(pos, yb, x1, route, modrows, par, alpha, mod_row):
    m, d = x1.shape
    nt = m // TM
    pos3 = pos.reshape(nt, 1, 2 * TM)

    def mod_map(t):
        return (mod_row(t), 0, 0)

    return pl.pallas_call(
        functools.partial(_combine_kernel, alpha=alpha),
        grid=(nt,),
        in_specs=[pl.BlockSpec((1, 1, 2 * TM), lambda t: (t, 0, 0), memory_space=pltpu.SMEM),
                  pl.BlockSpec((1, 1, 2 * TM), lambda t: (jnp.minimum(t + 1, nt - 1), 0, 0), memory_space=pltpu.SMEM),
                  pl.BlockSpec(memory_space=pl.ANY),
                  pl.BlockSpec((TM, d), lambda t: (t, 0)), pl.BlockSpec((TM, LANE), lambda t: (t, 0)),
                  pl.BlockSpec((1, 6, d), mod_map), _const_spec(par.shape)],
        out_specs=pl.BlockSpec((TM, d), lambda t: (t, 0)),
        out_shape=jax.ShapeDtypeStruct((m, d), F32),
        scratch_shapes=[pltpu.VMEM((2, 2 * TM, ROW_TILES, LANE), F32), pltpu.SemaphoreType.DMA((2,))],
        compiler_params=_cparams(),
        name="moe_combine",
    )(pos3, pos3, yb, x1, route, modrows, par)


def _rope_rot_perm():
    nf = QK_ROPE // 4
    src = np.zeros((QK_ROPE,), np.int32)
    sign = np.zeros((QK_ROPE,), np.float32)
    for a in range(2):
        for f in range(nf):
            i0 = a * 2 * nf + f
            i1 = a * 2 * nf + nf + f
            src[i0], sign[i0] = i1, -1.0
            src[i1], sign[i1] = i0, 1.0
    return src, sign


def _rope_tables(n_lat, n_ctx):
    nf = QK_ROPE // 4
    rows = n_lat // GRID_W
    inv = ROPE_BASE ** (-jnp.arange(nf, dtype=F32) / nf)
    r = jnp.repeat(jnp.arange(rows, dtype=F32), GRID_W)
    col = jnp.tile(jnp.arange(GRID_W, dtype=F32), rows)
    ang = jnp.stack([r[:, None] * inv, col[:, None] * inv], axis=1)
    cos = jnp.broadcast_to(jnp.cos(ang)[:, :, None, :], (n_lat, 2, 2, nf)).reshape(n_lat, QK_ROPE)
    sin = jnp.broadcast_to(jnp.sin(ang)[:, :, None, :], (n_lat, 2, 2, nf)).reshape(n_lat, QK_ROPE)
    cos = jnp.concatenate([jnp.ones((n_ctx, QK_ROPE), F32), cos], 0)
    sin = jnp.concatenate([jnp.zeros((n_ctx, QK_ROPE), F32), sin], 0)
    s = n_lat + n_ctx
    z = lambda w: jnp.zeros((s, w), F32)
    pad = LANE - QK_NOPE - QK_ROPE
    tqc = jnp.concatenate([jnp.full((s, QK_NOPE), MLA_SCALE, F32), cos * MLA_SCALE, z(pad)], 1)
    tqs = jnp.concatenate([z(QK_NOPE), sin * MLA_SCALE, z(pad)], 1)
    tkc = jnp.concatenate([z(QK_NOPE), cos, z(pad)], 1)
    tks = jnp.concatenate([z(QK_NOPE), sin, z(pad)], 1)
    return tqc, tqs, tkc, tks


def _head_pad_cols(w, width):
    k = w.shape[0]
    w = w.reshape(k, MLA_HEADS, width)
    return jnp.pad(w, ((0, 0), (0, 0), (0, HEAD_PAD - width))).reshape(k, MLA_HEADS * HEAD_PAD)


def _layer_weights(l, w_in, b_in, conv_a_w, conv_a_b, ln_a_g, ln_a_b, w_a_out, g_q, g_kv, w_uq, w_qr, w_uk, w_uv,
                   w_b_out, conv_c_w, a_log, dt_bias, g_o, w_c_out, w_out, ln1_g, ln1_b, w_rg, b_rg, w_re, b_re,
                   w1, w3, w2, ln2_g, ln2_b):
    src, sign = _rope_rot_perm()
    wi, bi = w_in[l], b_in[l][None, :]
    sizes = (2 * CONV_CH, Q_RANK, KV_RANK, QK_ROPE, 512, 512, 512, 512, 16, 16, 3 * D_MODEL)
    offs = np.concatenate([[0], np.cumsum(sizes)])

    def cols(a, i, j=None):
        return a[:, offs[i]:offs[(i if j is None else j) + 1]]

    def regroup(a):
        kr = cols(a, 3)
        z = lambda w: jnp.zeros((a.shape[0], w), a.dtype)
        kr_grp = jnp.concatenate([z(QK_NOPE), kr, z(32), z(QK_NOPE), kr[:, src] * sign, z(32)], 1)
        ba_grp = jnp.concatenate([cols(a, 8, 9), z(LANE - 32)], 1)
        return jnp.concatenate([cols(a, 0), cols(a, 1), cols(a, 2), kr_grp, cols(a, 4, 6), cols(a, 7), ba_grp,
                                cols(a, 10)], 1)

    w_all = regroup(wi).astype(BF16)
    b_all = regroup(bi)
    zl = jnp.zeros((LANE,), F32)
    gpar = jnp.stack([zl.at[16:32].set(a_log[l].reshape(-1)), zl.at[16:32].set(dt_bias[l].reshape(-1))], 0)
    gpar = jnp.concatenate([gpar, jnp.zeros((6, LANE), F32)], 0)

    wq = jnp.concatenate([w_uq[l].reshape(Q_RANK, MLA_HEADS, QK_NOPE), w_qr[l].reshape(Q_RANK, MLA_HEADS, QK_ROPE)], 2)
    wq = _head_pad_cols(wq.reshape(Q_RANK, -1), QK_NOPE + QK_ROPE).astype(BF16)
    wqr_rot = (w_qr[l].reshape(Q_RANK, MLA_HEADS, QK_ROPE)[:, :, src] * sign)
    wqr = jnp.concatenate([jnp.zeros((Q_RANK, MLA_HEADS, QK_NOPE), F32), wqr_rot], 2)
    wqr = _head_pad_cols(wqr.reshape(Q_RANK, -1), QK_NOPE + QK_ROPE).astype(BF16)
    wk = _head_pad_cols(w_uk[l], QK_NOPE).astype(BF16)
    wv = _head_pad_cols(w_uv[l], V_DIM).astype(BF16)
    vone = jnp.zeros((MLA_HEADS, HEAD_PAD), F32).at[:, V_DIM].set(1.0).reshape(1, -1)
    wb = jnp.pad(w_b_out[l].reshape(MLA_HEADS, V_DIM, D_MODEL), ((0, 0), (0, HEAD_PAD - V_DIM), (0, 0)))
    wb = wb.reshape(MLA_HEADS * HEAD_PAD, D_MODEL).astype(BF16)

    wr = jnp.concatenate([w_rg[l], w_re[l], jnp.zeros((D_MODEL, LANE - N_GROUPS - N_EXPERTS), F32)], 1)
    br = jnp.concatenate([b_rg[l], b_re[l], jnp.zeros((LANE - N_GROUPS - N_EXPERTS,), F32)])[None, :]
    return dict(
        w_all=w_all, b_all=b_all, gpar=gpar,
        conv_a_w=conv_a_w[l], conv_a_par=jnp.stack([conv_a_b[l], ln_a_g[l], ln_a_b[l]] + [jnp.zeros_like(ln_a_b[l])] * 5, 0),
        gq=g_q[l][None, :], gkv=g_kv[l][None, :], wq=wq, wqr=wqr, wk=wk, wv=wv, vone=vone,
        conv_c_w=conv_c_w[l],
        wa=w_a_out[l].astype(BF16), wb=wb, wc=w_c_out[l].astype(BF16), wo=w_out[l].astype(BF16),
        go=jnp.tile(g_o[l], GDN_HEADS)[None, :],
        ln1=jnp.stack([ln1_g[l], ln1_b[l]] + [jnp.zeros_like(ln1_g[l])] * 6, 0),
        ln2=jnp.stack([ln2_g[l], ln2_b[l]] + [jnp.zeros_like(ln2_g[l])] * 6, 0),
        wr=wr, br=br,
        w13=jnp.concatenate([w1[l], w3[l]], -1).astype(BF16), w2=w2[l].astype(BF16),
    )


def _routing_tables(route, cnt, n_tok):
    eid = route[:, 2:4].astype(jnp.int32)
    rank = route[:, 4:6].astype(jnp.int32)
    n_blk = 2 * n_tok // MOE_ROWS + N_EXPERTS
    sizes = cnt[0, N_GROUPS:N_GROUPS + N_EXPERTS].astype(jnp.int32)
    psizes = (sizes + MOE_ROWS - 1) // MOE_ROWS * MOE_ROWS
    pends = jnp.cumsum(psizes)
    pstarts = pends - psizes
    blk_start = jnp.arange(n_blk, dtype=jnp.int32) * MOE_ROWS
    blk_exp = jnp.minimum(jnp.sum((blk_start[:, None] >= pends[None, :]).astype(jnp.int32), axis=1), N_EXPERTS - 1)
    onehot = (eid[:, :, None] == jnp.arange(N_EXPERTS, dtype=jnp.int32)).astype(jnp.int32)
    pos = jnp.sum(onehot * pstarts, axis=-1) + rank
    tok = jnp.broadcast_to(jnp.arange(n_tok, dtype=jnp.int32)[:, None], (n_tok, 2))
    tok_buf = jnp.zeros((n_blk * MOE_ROWS,), jnp.int32).at[pos.reshape(-1)].set(tok.reshape(-1))
    return blk_exp, tok_buf, pos


def kernel(x, c, ctx, c_ctx, w_mod, b_mod, w_in, b_in, conv_a_w, conv_a_b, ln_a_g, ln_a_b, w_a_out, g_q, g_kv, w_uq,
           w_qr, w_uk, w_uv, w_b_out, conv_c_w, a_log, dt_bias, g_o, w_c_out, w_out, ln1_g, ln1_b, w_rg, b_rg, w_re,
           b_re, w1, w3, w2, ln2_g, ln2_b):
    n_batch, n_lat, d = x.shape
    n_ctx = ctx.shape[1]
    depth = w_mod.shape[0]
    alpha = (2 * depth) ** 0.25
    s_len = n_ctx + n_lat
    m = n_batch * s_len
    tpb = s_len // TM
    n_ctx_tiles = n_ctx // TM
    assert n_ctx % TM == 0 and n_lat % TM == 0 and n_ctx % CHUNK == 0 and s_len % n_ctx == 0

    xcat = jnp.concatenate([ctx, x], axis=1).reshape(m, d)
    cvec = jnp.concatenate([c, c_ctx[None, :], jnp.zeros((16 - n_batch - 1, d), F32)], 0)
    tabs = _rope_tables(n_lat, n_ctx)
    ones_blk = jnp.kron(jnp.eye(GDN_HEADS, dtype=F32), jnp.ones((GDN_DK, GDN_DK), F32)).astype(BF16)
    params = (w_in, b_in, conv_a_w, conv_a_b, ln_a_g, ln_a_b, w_a_out, g_q, g_kv, w_uq, w_qr, w_uk, w_uv, w_b_out,
              conv_c_w, a_log, dt_bias, g_o, w_c_out, w_out, ln1_g, ln1_b, w_rg, b_rg, w_re, b_re, w1, w3, w2,
              ln2_g, ln2_b)
    tq = TM

    for l in range(depth):
        p = _layer_weights(l, *params)
        mod = _mod_call(cvec, w_mod[l].astype(BF16), b_mod[l][None, :])
        modrows = mod[:n_batch + 1].reshape(n_batch + 1, 6, d)

        u, qd, kvd, kr, gqkv, gg, ba, mg = _inproj_call(xcat, modrows, p["w_all"], p["b_all"], p["gpar"], tpb,
                                                        n_ctx_tiles)
        za = _conv_a_call(u, p["conv_a_w"], p["conv_a_par"], tpb, n_ctx_tiles)
        qp, kp, vp = _mla_proj_call(qd, kvd, kr, tabs, p["gq"], p["gkv"], p["wq"], p["wqr"], p["wk"], p["wv"],
                                    p["vone"], tpb)
        last = l == depth - 1
        zb = _attn_call(qp, kp, vp, n_batch, s_len, n_ctx, n_lat, s_len, tq)
        if not last:
            zb_c = _attn_call(qp, kp, vp, n_batch, s_len, 0, n_ctx, n_ctx, n_ctx)
            hp = MLA_HEADS * HEAD_PAD
            zb = jnp.concatenate([zb_c.reshape(n_batch, n_ctx, hp), zb.reshape(n_batch, n_lat, hp)], 1).reshape(m, hp)
        qkvn = _gdn_prep_call(gqkv, p["conv_c_w"], ones_blk, tpb, n_ctx_tiles)
        bgt2 = ba[:, :32].reshape(m // CHUNK, CHUNK, 2, 2, N_PAIRS, 2).transpose(0, 2, 3, 4, 5, 1)
        bgt2 = bgt2.reshape(m // CHUNK, 16, LANE)
        gw, gu0, gkd, gqg, gpm, gcx = _gdn_local_call(qkvn, ba, bgt2)
        of, ob = _gdn_scan_call(gw, gu0, gkd, gqg, gpm, gcx, n_batch, tpb, n_ctx_tiles)
        x1 = _merge_call(za, zb, of, ob, gg, mg, xcat, modrows, p["wa"], p["wb"], p["wc"], p["wo"], ones_blk, p["ln1"],
                         p["go"], alpha, tpb, n_ctx_tiles, last)
        n_tok = x1.shape[0]
        _, _, mod_row = _tile_maps(n_batch, tpb, n_ctx_tiles, last)
        h3, route, cnt = _router_call(x1, modrows, p["wr"], p["br"], mod_row)
        blk_exp, tok_buf, pos = _routing_tables(route, cnt, n_tok)
        yb = _expert_call(blk_exp, tok_buf, h3, p["w13"], p["w2"])
        pos_t = pos.reshape(n_tok // TM, TM, 2).transpose(0, 2, 1).reshape(-1)
        xcat = _combine_call(pos_t, yb, x1, route, modrows, p["ln2"], alpha, mod_row)

    return xcat.reshape(n_batch, n_lat, d)
```

```python
import functools

import jax
import jax.numpy as jnp
import numpy as np
from jax import lax
from jax.experimental import pallas as pl
from jax.experimental.pallas import tpu as pltpu

F32 = jnp.float32
BF16 = jnp.bfloat16

D_MODEL = 1024
GRID_W = 64
EPS = 1e-6
CONV_CH = 512
CONV_K = 31
MLA_HEADS = 8
QK_NOPE = 64
QK_ROPE = 32
V_DIM = 64
Q_RANK = 384
KV_RANK = 256
ROPE_BASE = 10000.0
MLA_SCALE = (QK_NOPE + QK_ROPE) ** -0.5
GDN_HEADS = 8
GDN_DK = 64
GDN_DV = 64
SHORT_K = 5
CHUNK = 64
N_GROUPS = 4
EXP_PER_GROUP = 8
N_EXPERTS = N_GROUPS * EXP_PER_GROUP
D_EXPERT = 256

LANE = 128
TM = 256
HEAD_PAD = LANE
MOE_ROWS = 512
VMEM_LIMIT = 56 * 1024 * 1024

OFF_GLU = 0
OFF_QD = OFF_GLU + 2 * CONV_CH
OFF_KVD = OFF_QD + Q_RANK
OFF_KR = OFF_KVD + KV_RANK
OFF_GQKV = OFF_KR + 2 * LANE
OFF_GG = OFF_GQKV + 3 * GDN_HEADS * GDN_DK
OFF_BA = OFF_GG + GDN_HEADS * GDN_DV
OFF_MG = OFF_BA + LANE
P_ALL = OFF_MG + 3 * D_MODEL


def _cparams(n_grid=1):
    return pltpu.CompilerParams(dimension_semantics=("arbitrary",) * n_grid, vmem_limit_bytes=VMEM_LIMIT)


def _const_spec(shape):
    nd = len(shape)
    return pl.BlockSpec(shape, lambda *_: (0,) * nd, pipeline_mode=pl.Buffered(1))


def _bdot(a, b):
    return jnp.dot(a.astype(BF16), b.astype(BF16), preferred_element_type=F32)


def _hdot(a, b):
    return jnp.dot(a, b, precision=lax.Precision.HIGHEST, preferred_element_type=F32)


def _dot_nt(a, b):
    return lax.dot_general(a.astype(BF16), b.astype(BF16), (((1,), (1,)), ((), ())), preferred_element_type=F32)


def _dot_tn(a, b):
    return lax.dot_general(a.astype(BF16), b.astype(BF16), (((0,), (0,)), ((), ())), preferred_element_type=F32)


def _silu(x):
    return x * jax.nn.sigmoid(x)


def _group_sumsq(x, ones_blk):
    x2 = x * x
    hi = x2.astype(BF16)
    lo = (x2 - hi.astype(F32)).astype(BF16)
    return jnp.dot(hi, ones_blk, preferred_element_type=F32) + jnp.dot(lo, ones_blk, preferred_element_type=F32)


def _layernorm(x, g, b):
    mu = jnp.mean(x, -1, keepdims=True)
    xc = x - mu
    var = jnp.mean(xc * xc, -1, keepdims=True)
    return xc * lax.rsqrt(var + EPS) * g + b


def _mod_kernel(c_ref, w_ref, b_ref, o_ref):
    o_ref[...] = _bdot(_silu(c_ref[...]), w_ref[...]) + b_ref[...]


def _mod_call(cvec, w, b):
    rows, d = cvec.shape
    n = w.shape[1]
    tn = 1536
    return pl.pallas_call(
        _mod_kernel,
        grid=(n // tn,),
        in_specs=[pl.BlockSpec((rows, d), lambda j: (0, 0)),
                  pl.BlockSpec((d, tn), lambda j: (0, j)),
                  pl.BlockSpec((1, tn), lambda j: (0, j))],
        out_specs=pl.BlockSpec((rows, tn), lambda j: (0, j)),
        out_shape=jax.ShapeDtypeStruct((rows, n), F32),
        compiler_params=_cparams(),
        name="mod_vectors",
    )(cvec, w, b)


def _inproj_kernel(x_ref, mod_ref, w_ref, b_ref, gpar_ref,
                   u_ref, qd_ref, kvd_ref, kr_ref, gqkv_ref, gg_ref, ba_ref, mg_ref):
    x = x_ref[...]
    sh = mod_ref[0, 0:1, :]
    sc = mod_ref[0, 1:2, :]
    h = (x * (1.0 + sc) + sh).astype(BF16)

    def proj(off, n):
        return jnp.dot(h, w_ref[:, off:off + n], preferred_element_type=F32) + b_ref[:, off:off + n]

    y = proj(OFF_GLU, 2 * CONV_CH)
    u_ref[...] = y[:, :CONV_CH] * jax.nn.sigmoid(y[:, CONV_CH:])
    qd_ref[...] = proj(OFF_QD, Q_RANK)
    kvd_ref[...] = proj(OFF_KVD, KV_RANK)
    kr_ref[...] = proj(OFF_KR, 2 * LANE)
    gqkv_ref[...] = proj(OFF_GQKV, 3 * GDN_HEADS * GDN_DK)
    gg_ref[...] = proj(OFF_GG, GDN_HEADS * GDN_DV)
    y = proj(OFF_BA, LANE)
    lane = lax.broadcasted_iota(jnp.int32, y.shape, 1)
    z = y + gpar_ref[1:2, :]
    sp = jnp.maximum(z, 0.0) + jnp.log1p(jnp.exp(-jnp.abs(z)))
    ba_ref[...] = jnp.where(lane < 2 * GDN_HEADS, jax.nn.sigmoid(y), -jnp.exp(gpar_ref[0:1, :]) * sp)
    mg_ref[...] = jax.nn.sigmoid(proj(OFF_MG, 3 * D_MODEL)).astype(BF16)


def _inproj_call(xcat, modrows, w_all, b_all, gpar, tpb, n_ctx_tiles):
    m, d = xcat.shape
    nt = m // TM
    n_batch = modrows.shape[0] - 1

    def mod_map(t):
        return (jnp.where(t % tpb < n_ctx_tiles, n_batch, t // tpb), 0, 0)

    widths = (CONV_CH, Q_RANK, KV_RANK, 2 * LANE, 3 * GDN_HEADS * GDN_DK, GDN_HEADS * GDN_DV, LANE, 3 * D_MODEL)
    dtypes = (F32, F32, F32, F32, F32, F32, F32, BF16)
    return pl.pallas_call(
        _inproj_kernel,
        grid=(nt,),
        in_specs=[pl.BlockSpec((TM, d), lambda t: (t, 0)),
                  pl.BlockSpec((1, 6, d), mod_map),
                  _const_spec(w_all.shape), _const_spec(b_all.shape), _const_spec(gpar.shape)],
        out_specs=[pl.BlockSpec((TM, w), lambda t: (t, 0)) for w in widths],
        out_shape=[jax.ShapeDtypeStruct((m, w), dt) for w, dt in zip(widths, dtypes)],
        compiler_params=_cparams(),
        name="in_proj",
    )(xcat, modrows, w_all, b_all, gpar)


CONV_HALO = 16


def _conv_a_kernel(top_ref, cur_ref, bot_ref, w_ref, par_ref, o_ref, win_ref, *, tpb, n_ctx_tiles):
    t = pl.program_id(0)
    r = t % tpb
    top_ok = jnp.logical_and(r != 0, r != n_ctx_tiles)
    bot_ok = jnp.logical_and(r != n_ctx_tiles - 1, r != tpb - 1)
    win_ref[0:CONV_HALO, :] = jnp.where(top_ok, top_ref[...], 0.0)
    win_ref[CONV_HALO:CONV_HALO + TM, :] = cur_ref[...]
    win_ref[CONV_HALO + TM:, :] = jnp.where(bot_ok, bot_ref[...], 0.0)
    acc = jnp.zeros((TM, CONV_CH), F32)
    base = CONV_HALO - CONV_K // 2
    for k in range(CONV_K):
        acc = acc + win_ref[pl.ds(base + k, TM), :] * w_ref[k:k + 1, :]
    y = _layernorm(acc + par_ref[0:1, :], par_ref[1:2, :], par_ref[2:3, :])
    o_ref[...] = _silu(y).astype(BF16)


def _conv_a_call(u, w, par, tpb, n_ctx_tiles):
    m, c = u.shape
    nt = m // TM
    hb = TM // CONV_HALO
    last = m // CONV_HALO - 1
    wp = jnp.zeros((32, c), F32).at[:CONV_K].set(w)
    return pl.pallas_call(
        functools.partial(_conv_a_kernel, tpb=tpb, n_ctx_tiles=n_ctx_tiles),
        grid=(nt,),
        in_specs=[pl.BlockSpec((CONV_HALO, c), lambda t: (jnp.maximum(t * hb - 1, 0), 0)),
                  pl.BlockSpec((TM, c), lambda t: (t, 0)),
                  pl.BlockSpec((CONV_HALO, c), lambda t: (jnp.minimum((t + 1) * hb, last), 0)),
                  _const_spec(wp.shape), _const_spec(par.shape)],
        out_specs=pl.BlockSpec((TM, c), lambda t: (t, 0)),
        out_shape=jax.ShapeDtypeStruct((m, c), BF16),
        scratch_shapes=[pltpu.VMEM((TM + 2 * CONV_HALO, c), F32)],
        compiler_params=_cparams(),
        name="conv_branch",
    )(u, u, u, wp, par)


def _mla_proj_kernel(qd_ref, kvd_ref, kr_ref, tqc_ref, tqs_ref, tkc_ref, tks_ref, gq_ref, gkv_ref,
                     wq_ref, wqr_ref, wk_ref, wv_ref, vone_ref, q_ref, k_ref, v_ref):
    qd = qd_ref[...]
    cq = qd * lax.rsqrt(jnp.mean(qd * qd, -1, keepdims=True) + EPS) * gq_ref[...]
    qf = _bdot(cq, wq_ref[...])
    qr = _bdot(cq, wqr_ref[...])
    kvd = kvd_ref[...]
    ckv = kvd * lax.rsqrt(jnp.mean(kvd * kvd, -1, keepdims=True) + EPS) * gkv_ref[...]
    kn = _bdot(ckv, wk_ref[...])
    krf = kr_ref[:, :LANE] * tkc_ref[...] + kr_ref[:, LANE:] * tks_ref[...]
    tqc = tqc_ref[...]
    tqs = tqs_ref[...]
    for h in range(MLA_HEADS):
        sl = slice(h * HEAD_PAD, (h + 1) * HEAD_PAD)
        q_ref[:, sl] = (qf[:, sl] * tqc + qr[:, sl] * tqs).astype(BF16)
        k_ref[:, sl] = (kn[:, sl] + krf).astype(BF16)
    v_ref[...] = (_bdot(ckv, wv_ref[...]) + vone_ref[...]).astype(BF16)


def _mla_proj_call(qd, kvd, kr, tabs, gq, gkv, wq, wqr, wk, wv, vone, tpb):
    m = qd.shape[0]
    nt = m // TM
    hp = MLA_HEADS * HEAD_PAD
    tab_spec = pl.BlockSpec((TM, LANE), lambda t: (t % tpb, 0))
    row = lambda w: pl.BlockSpec((TM, w), lambda t: (t, 0))
    return pl.pallas_call(
        _mla_proj_kernel,
        grid=(nt,),
        in_specs=[row(Q_RANK), row(KV_RANK), row(2 * LANE), tab_spec, tab_spec, tab_spec, tab_spec,
                  _const_spec(gq.shape), _const_spec(gkv.shape), _const_spec(wq.shape), _const_spec(wqr.shape),
                  _const_spec(wk.shape), _const_spec(wv.shape), _const_spec(vone.shape)],
        out_specs=[row(hp), row(hp), row(hp)],
        out_shape=[jax.ShapeDtypeStruct((m, hp), BF16)] * 3,
        compiler_params=_cparams(),
        name="mla_proj",
    )(qd, kvd, kr, *tabs, gq, gkv, wq, wqr, wk, wv, vone)


def _attn_kernel(q_ref, k_ref, v_ref, o_ref):
    for h in range(MLA_HEADS):
        sl = slice(h * HEAD_PAD, (h + 1) * HEAD_PAD)
        s = lax.dot_general(q_ref[:, sl], k_ref[:, sl], (((1,), (1,)), ((), ())), preferred_element_type=F32)
        m = jnp.max(s, axis=-1, keepdims=True)
        p = jnp.exp(s - m).astype(BF16)
        o = jnp.dot(p, v_ref[:, sl], preferred_element_type=F32)
        denom = o[:, V_DIM:V_DIM + 1]
        o_ref[:, sl] = (o * (1.0 / denom)).astype(BF16)


def _attn_call(q, k, v, n_batch, s_len, q_start, q_len, k_len, tq):
    hp = MLA_HEADS * HEAD_PAD
    nq = q_len // tq
    q0 = q_start // tq
    spb = s_len // tq
    kb = s_len // k_len
    return pl.pallas_call(
        _attn_kernel,
        grid=(n_batch, nq),
        in_specs=[pl.BlockSpec((tq, hp), lambda b, i: (b * spb + q0 + i, 0)),
                  pl.BlockSpec((k_len, hp), lambda b, i: (b * kb, 0), pipeline_mode=pl.Buffered(1)),
                  pl.BlockSpec((k_len, hp), lambda b, i: (b * kb, 0), pipeline_mode=pl.Buffered(1))],
        out_specs=pl.BlockSpec((tq, hp), lambda b, i: (b * nq + i, 0)),
        out_shape=jax.ShapeDtypeStruct((n_batch * q_len, hp), BF16),
        compiler_params=_cparams(2),
        name="attention",
    )(q, k, v)


GDN_HALO = 8


def _gdn_prep_kernel(top_ref, cur_ref, bot_ref, w_ref, ones_ref, o_ref, win_ref, *, tpb, n_ctx_tiles):
    t = pl.program_id(0)
    r = t % tpb
    top_ok = jnp.logical_and(r != 0, r != n_ctx_tiles)
    bot_ok = jnp.logical_and(r != n_ctx_tiles - 1, r != tpb - 1)
    win_ref[0:GDN_HALO, :] = jnp.where(top_ok, top_ref[...], 0.0)
    win_ref[GDN_HALO:GDN_HALO + TM, :] = cur_ref[...]
    win_ref[GDN_HALO + TM:, :] = jnp.where(bot_ok, bot_ref[...], 0.0)
    c = 3 * GDN_HEADS * GDN_DK
    acc = jnp.zeros((TM, c), F32)
    base = GDN_HALO - SHORT_K // 2
    for k in range(SHORT_K):
        acc = acc + win_ref[pl.ds(base + k, TM), :] * w_ref[k:k + 1, :]
    u = _silu(acc)
    qk = GDN_HEADS * GDN_DK
    q = u[:, :qk]
    kk = u[:, qk:2 * qk]
    ones_blk = ones_ref[...]
    q = q * lax.rsqrt(_group_sumsq(q, ones_blk) + EPS) * (GDN_DK ** -0.5)
    kk = kk * lax.rsqrt(_group_sumsq(kk, ones_blk) + EPS)
    o_ref[:, :qk] = q.astype(BF16)
    o_ref[:, qk:2 * qk] = kk.astype(BF16)
    o_ref[:, 2 * qk:] = u[:, 2 * qk:].astype(BF16)


def _gdn_prep_call(gqkv, w, ones_blk, tpb, n_ctx_tiles):
    m, c = gqkv.shape
    nt = m // TM
    hb = TM // GDN_HALO
    last = m // GDN_HALO - 1
    wp = jnp.zeros((8, c), F32).at[:SHORT_K].set(w)
    return pl.pallas_call(
        functools.partial(_gdn_prep_kernel, tpb=tpb, n_ctx_tiles=n_ctx_tiles),
        grid=(nt,),
        in_specs=[pl.BlockSpec((GDN_HALO, c), lambda t: (jnp.maximum(t * hb - 1, 0), 0)),
                  pl.BlockSpec((TM, c), lambda t: (t, 0)),
                  pl.BlockSpec((GDN_HALO, c), lambda t: (jnp.minimum((t + 1) * hb, last), 0)),
                  _const_spec(wp.shape), _const_spec(ones_blk.shape)],
        out_specs=pl.BlockSpec((TM, c), lambda t: (t, 0)),
        out_shape=jax.ShapeDtypeStruct((m, c), BF16),
        scratch_shapes=[pltpu.VMEM((TM + 2 * GDN_HALO, c), F32)],
        compiler_params=_cparams(),
        name="gdn_prep",
    )(gqkv, gqkv, gqkv, wp, ones_blk)


N_PAIRS = GDN_HEADS // 2
CPT = TM // CHUNK


def _left_half(shape):
    return lax.broadcasted_iota(jnp.int32, shape, len(shape) - 1) < GDN_DK


def _blockdiag(x2):
    left = _left_half(x2.shape)
    zero = jnp.zeros_like(x2)
    return jnp.concatenate([jnp.where(left, x2, zero), jnp.where(left, zero, x2)], axis=0).astype(BF16)


def _blockdiag_pair(x):
    return jnp.concatenate([_blockdiag(x[:, :LANE]), _blockdiag(x[:, LANE:])], axis=1)


def _hi_lo(x):
    hi = x.astype(BF16)
    return hi, (x - hi.astype(F32)).astype(BF16)


def _lane_bcast_pair(x, c0):
    shape = (x.shape[0], LANE)
    return jnp.where(_left_half(shape), jnp.broadcast_to(x[:, c0:c0 + 1], shape),
                     jnp.broadcast_to(x[:, c0 + 1:c0 + 2], shape))


def _gdn_local_kernel(qkv_ref, bg_ref, bgt_ref, w_ref, u0_ref, kd_ref, qg_ref, pm_ref, gcx_ref):
    qk = GDN_HEADS * GDN_DK
    nh2 = 2 * GDN_HEADS
    ri = lax.broadcasted_iota(jnp.int32, (CHUNK, LANE), 0)
    ci = lax.broadcasted_iota(jnp.int32, (CHUNK, LANE), 1) % CHUNK
    incl = (ri >= ci, ri <= ci)
    strict = (ri > ci, ri < ci)
    eye2 = (ri == ci).astype(F32)
    r64 = lax.broadcasted_iota(jnp.int32, (CHUNK, CHUNK), 0)
    c64 = lax.broadcasted_iota(jnp.int32, (CHUNK, CHUNK), 1)
    m_incl = ((r64 >= c64).astype(F32), (r64 <= c64).astype(F32))
    rj = lax.broadcasted_iota(jnp.int32, (LANE, LANE), 0)
    cl = lax.broadcasted_iota(jnp.int32, (LANE, LANE), 1)
    same = (rj // CHUNK) == (cl // CHUNK)
    m_incl_t = (jnp.logical_and(same, rj % CHUNK <= cl % CHUNK).astype(F32),
                jnp.logical_and(same, rj % CHUNK >= cl % CHUNK).astype(F32))
    ones_bd = same.astype(F32)

    chains = []
    for j in range(CPT):
        rows = slice(j * CHUNK, (j + 1) * CHUNK)
        bg = bg_ref[rows, :]
        bgt = bgt_ref[j]
        gam = (_hdot(m_incl[0], bg), _hdot(m_incl[1], bg))
        gam_t = (_hdot(bgt, m_incl_t[0]), _hdot(bgt, m_incl_t[1]))
        gt_rows = _hdot(bgt, ones_bd)
        gcx_ref[j * 8:(j + 1) * 8, :] = jnp.exp(gt_rows[8:16])
        for p in range(N_PAIRS):
            ls = slice(p * LANE, (p + 1) * LANE)
            q2 = qkv_ref[rows, p * LANE:(p + 1) * LANE]
            k2 = qkv_ref[rows, qk + p * LANE:qk + (p + 1) * LANE]
            v2 = qkv_ref[rows, 2 * qk + p * LANE:2 * qk + (p + 1) * LANE]
            left = _left_half(k2.shape)
            zero = jnp.zeros_like(k2)
            k_bd = jnp.concatenate([jnp.where(left, k2, zero), jnp.where(left, zero, k2)], axis=0)
            kq = lax.dot_general(jnp.concatenate([k2, q2], axis=0), k_bd, (((1,), (1,)), ((), ())),
                                 preferred_element_type=F32)
            kk2, qk2 = kq[:CHUNK], kq[CHUNK:]
            k2f, q2f, v2f = k2.astype(F32), q2.astype(F32), v2.astype(F32)
            for d in range(2):
                c0 = d * GDN_HEADS + 2 * p
                r = 8 + d * N_PAIRS + p
                b2 = _lane_bcast_pair(bg, c0)
                gcol2 = _lane_bcast_pair(gam[d], nh2 + c0)
                grow2 = gam_t[d][r:r + 1, :]
                gt2 = gt_rows[r:r + 1, :]
                decay2 = jnp.where(incl[d], jnp.exp(jnp.where(incl[d], gcol2 - grow2, 0.0)), 0.0)
                n2 = -jnp.where(strict[d], b2 * decay2 * kk2, 0.0)
                eg2 = jnp.exp(gcol2)
                rhs = jnp.concatenate([(b2 * eg2) * k2f, b2 * v2f], axis=1)
                kd_ref[d, rows, ls] = (k2f * jnp.exp(gt2 - gcol2)).astype(BF16)
                qg_ref[d, rows, ls] = (q2f * eg2).astype(BF16)
                pm_ref[d, rows, ls] = (decay2 * qk2).astype(BF16)
                chains.append(dict(d=d, rows=rows, ls=ls, n=n2, pw=n2, t=eye2 + n2, rhs=rhs))

    for ch in chains:
        ch["pw"] = jnp.dot(ch["pw"].astype(BF16), _blockdiag(ch["pw"]), preferred_element_type=F32)
    for _ in range(4):
        for ch in chains:
            res = jnp.dot(jnp.concatenate([ch["pw"], ch["t"]], axis=0).astype(BF16), _blockdiag(ch["pw"]),
                          preferred_element_type=F32)
            ch["pw"] = res[:CHUNK]
            ch["t"] = ch["t"] + res[CHUNK:]
    for ch in chains:
        ch["t"] = ch["t"] + jnp.dot(ch["t"].astype(BF16), _blockdiag(ch["pw"]), preferred_element_type=F32)
    for ch in chains:
        ch["tb"] = ch["t"].astype(BF16)
        ch["x"] = jnp.dot(ch["tb"], _blockdiag_pair(ch["rhs"]), preferred_element_type=F32)
    for ch in chains:
        n_hi, n_lo = _hi_lo(ch["n"])
        x_hi, x_lo = _hi_lo(ch["x"])
        top = jnp.dot(jnp.concatenate([n_hi, n_lo], axis=0), _blockdiag_pair(x_hi), preferred_element_type=F32)
        nx = top[:CHUNK] + top[CHUNK:] + jnp.dot(n_hi, _blockdiag_pair(x_lo), preferred_element_type=F32)
        ch["r"] = ch["rhs"] - ch["x"] + nx
    for ch in chains:
        x = ch["x"] + jnp.dot(ch["tb"], _blockdiag_pair(ch["r"]), preferred_element_type=F32)
        w_ref[ch["d"], ch["rows"], ch["ls"]] = x[:, :LANE].astype(BF16)
        u0_ref[ch["d"], ch["rows"], ch["ls"]] = x[:, LANE:]


def _gdn_local_call(qkvn, bg, bgt2):
    m, c = qkvn.shape
    nt = m // TM
    dv = GDN_HEADS * GDN_DV
    spec2 = pl.BlockSpec((2, TM, dv), lambda t: (0, t, 0))
    shp = lambda dt: jax.ShapeDtypeStruct((2, m, dv), dt)
    return pl.pallas_call(
        _gdn_local_kernel,
        grid=(nt,),
        in_specs=[pl.BlockSpec((TM, c), lambda t: (t, 0)), pl.BlockSpec((TM, LANE), lambda t: (t, 0)),
                  pl.BlockSpec((CPT, 16, LANE), lambda t: (t, 0, 0))],
        out_specs=[spec2, spec2, spec2, spec2, spec2, pl.BlockSpec((CPT * 8, LANE), lambda t: (t, 0))],
        out_shape=[shp(BF16), shp(F32), shp(BF16), shp(BF16), shp(BF16),
                   jax.ShapeDtypeStruct((m // CHUNK * 8, LANE), F32)],
        compiler_params=_cparams(),
        name="gdn_local",
    )(qkvn, bg, bgt2)


def _gdn_scan_kernel(wf, u0f, kdf, qgf, pmf, gcf, wb, u0b, kdb, qgb, pmb, gcb, of_ref, ob_ref, s_ref):
    @pl.when(pl.program_id(1) == 0)
    def _():
        s_ref[...] = jnp.zeros_like(s_ref)

    refs = ((wf, u0f, kdf, qgf, pmf, gcf, of_ref), (wb, u0b, kdb, qgb, pmb, gcb, ob_ref))
    left = _left_half((CHUNK, LANE))
    state = [s_ref[i] for i in range(2 * N_PAIRS)]
    for sub in range(CPT):
        cur = []
        for d in range(2):
            j = sub if d == 0 else CPT - 1 - sub
            rows = slice(j * CHUNK, (j + 1) * CHUNK)
            for p in range(N_PAIRS):
                cur.append((d, p, rows, slice(p * LANE, (p + 1) * LANE), j * 8 + d * N_PAIRS + p))
        res = []
        for d, p, rows, ls, gr in cur:
            w_r, _, _, qg_r, _, _, _ = refs[d]
            lhs = jnp.concatenate([w_r[rows, ls], qg_r[rows, ls]], axis=0)
            res.append(jnp.dot(lhs, _blockdiag(state[d * N_PAIRS + p]), preferred_element_type=F32))
        us = []
        for (d, p, rows, ls, gr), r in zip(cur, res):
            us.append(refs[d][1][rows, ls] - r[:CHUNK])
        for (d, p, rows, ls, gr), r, u in zip(cur, res, us):
            _, _, kd_r, _, pm_r, gc_r, o_r = refs[d]
            o_r[rows, ls] = r[CHUNK:] + jnp.dot(pm_r[rows, ls], _blockdiag(u), preferred_element_type=F32)
            upd = lax.dot_general(kd_r[rows, ls], u.astype(BF16), (((0,), (0,)), ((), ())),
                                  preferred_element_type=F32)
            i = d * N_PAIRS + p
            state[i] = gc_r[gr:gr + 1, :] * state[i] + jnp.where(left, upd[:CHUNK], upd[CHUNK:])
    for i in range(2 * N_PAIRS):
        s_ref[i] = state[i]


def _gdn_scan_call(w, u0, kd, qg, pm, gcx, n_batch, tpb, n_ctx_tiles):
    m, dv = w.shape[1], w.shape[2]

    def fwd_tile(b, j):
        return b * tpb + j

    def bwd_tile(b, j):
        return b * tpb + jnp.where(j < n_ctx_tiles, n_ctx_tiles - 1 - j, tpb - 1 + n_ctx_tiles - j)

    def specs(d, tile):
        big = pl.BlockSpec((None, TM, dv), lambda b, j: (d, tile(b, j), 0))
        return [big, big, big, big, big, pl.BlockSpec((CPT * 8, LANE), lambda b, j: (tile(b, j), 0))]

    return pl.pallas_call(
        _gdn_scan_kernel,
        grid=(n_batch, tpb),
        in_specs=specs(0, fwd_tile) + specs(1, bwd_tile),
        out_specs=[pl.BlockSpec((TM, dv), lambda b, j: (fwd_tile(b, j), 0)),
                   pl.BlockSpec((TM, dv), lambda b, j: (bwd_tile(b, j), 0))],
        out_shape=[jax.ShapeDtypeStruct((m, dv), F32)] * 2,
        scratch_shapes=[pltpu.VMEM((2 * N_PAIRS, GDN_DK, LANE), F32)],
        compiler_params=_cparams(2),
        name="gdn_scan",
    )(w, u0, kd, qg, pm, gcx, w, u0, kd, qg, pm, gcx)


def _merge_kernel(za_ref, zb_ref, of_ref, ob_ref, gg_ref, mg_ref, x_ref, mod_ref, wa_ref, wb_ref, wc_ref, wo_ref,
                  ones_ref, par_ref, go_ref, o_ref, *, alpha):
    oc = of_ref[...] + ob_ref[...]
    ms = _group_sumsq(oc, ones_ref[...]) * (1.0 / GDN_DV)
    zc = oc * lax.rsqrt(ms + EPS) * go_ref[...] * _silu(gg_ref[...])
    ya = jnp.dot(za_ref[...], wa_ref[...], preferred_element_type=F32)
    yb = jnp.dot(zb_ref[...], wb_ref[...], preferred_element_type=F32)
    yc = _bdot(zc, wc_ref[...])
    d = D_MODEL
    mix = (mg_ref[:, 0:d].astype(F32) * ya + mg_ref[:, d:2 * d].astype(F32) * yb
           + mg_ref[:, 2 * d:3 * d].astype(F32) * yc)
    y = _bdot(mix, wo_ref[...])
    g1 = mod_ref[0, 2:3, :]
    o_ref[...] = _layernorm(alpha * x_ref[...] + g1 * y, par_ref[0:1, :], par_ref[1:2, :])


def _tile_maps(n_batch, tpb, n_ctx_tiles, lat_only):
    if lat_only:
        lpb = tpb - n_ctx_tiles
        return n_batch * lpb, (lambda t: (t // lpb) * tpb + n_ctx_tiles + t % lpb), (lambda t: t // lpb)
    return n_batch * tpb, (lambda t: t), (lambda t: jnp.where(t % tpb < n_ctx_tiles, n_batch, t // tpb))


def _merge_call(za, zb, of, ob, gg, mg, xcat, modrows, wa, wb, wc, wo, ones_blk, par, go, alpha, tpb, n_ctx_tiles,
                lat_only):
    d = xcat.shape[1]
    nt, src_tile, mod_row = _tile_maps(modrows.shape[0] - 1, tpb, n_ctx_tiles, lat_only)
    row = lambda w: pl.BlockSpec((TM, w), lambda t: (src_tile(t), 0))
    own = lambda w: pl.BlockSpec((TM, w), lambda t: (t, 0))
    return pl.pallas_call(
        functools.partial(_merge_kernel, alpha=alpha),
        grid=(nt,),
        in_specs=[row(za.shape[1]), own(zb.shape[1]), row(of.shape[1]), row(ob.shape[1]), row(gg.shape[1]),
                  row(mg.shape[1]), row(d),
                  pl.BlockSpec((1, 6, d), lambda t: (mod_row(t), 0, 0)),
                  _const_spec(wa.shape), _const_spec(wb.shape), _const_spec(wc.shape), _const_spec(wo.shape),
                  _const_spec(ones_blk.shape), _const_spec(par.shape), _const_spec(go.shape)],
        out_specs=own(d),
        out_shape=jax.ShapeDtypeStruct((nt * TM, d), F32),
        compiler_params=_cparams(),
        name="merge",
    )(za, zb, of, ob, gg, mg, xcat, modrows, wa, wb, wc, wo, ones_blk, par, go)


ROW_TILES = D_MODEL // LANE


def _to_token_tiles(x):
    chunks = jnp.stack([x[:, s * LANE:(s + 1) * LANE] for s in range(ROW_TILES)], axis=0)
    return pltpu.einshape("stl->tsl", chunks)


def _from_token_tiles(x3):
    chunks = pltpu.einshape("tsl->stl", x3)
    return [chunks[s] for s in range(ROW_TILES)]


def _router_kernel(x_ref, mod_ref, w_ref, b_ref, h_ref, r_ref, cnt_ref, cnt_scr):
    @pl.when(pl.program_id(0) == 0)
    def _():
        cnt_scr[...] = jnp.zeros_like(cnt_scr)

    sh = mod_ref[0, 3:4, :]
    sc = mod_ref[0, 4:5, :]
    h = x_ref[...] * (1.0 + sc) + sh
    h_ref[...] = _to_token_tiles(h)
    h_hi, h_lo = _hi_lo(h)
    lg = (jnp.dot(h_hi, w_ref[0], preferred_element_type=F32) + jnp.dot(h_lo, w_ref[0], preferred_element_type=F32)
          + jnp.dot(h_hi, w_ref[1], preferred_element_type=F32)) + b_ref[...]
    lane = lax.broadcasted_iota(jnp.int32, lg.shape, 1)
    neg = jnp.float32(-1e30)
    big = jnp.int32(1 << 20)
    is_g = lane < N_GROUPS
    gmax = jnp.max(jnp.where(is_g, lg, neg), axis=-1, keepdims=True)
    gsum = jnp.sum(jnp.where(is_g, jnp.exp(lg - gmax), 0.0), axis=-1, keepdims=True)
    gp = 1.0 / gsum
    gi = jnp.min(jnp.where(jnp.logical_and(is_g, lg == gmax), lane, big), axis=-1, keepdims=True)
    lo = N_GROUPS + gi * EXP_PER_GROUP
    in_grp = jnp.logical_and(lane >= lo, lane < lo + EXP_PER_GROUP)
    v1 = jnp.max(jnp.where(in_grp, lg, neg), axis=-1, keepdims=True)
    l1 = jnp.min(jnp.where(jnp.logical_and(in_grp, lg == v1), lane, big), axis=-1, keepdims=True)
    rest = jnp.logical_and(in_grp, lane != l1)
    v2 = jnp.max(jnp.where(rest, lg, neg), axis=-1, keepdims=True)
    l2 = jnp.min(jnp.where(jnp.logical_and(rest, lg == v2), lane, big), axis=-1, keepdims=True)
    e2 = jnp.exp(v2 - v1)
    p1 = 1.0 / (1.0 + e2)
    p2 = e2 / (1.0 + e2)
    out = jnp.where(lane == 0, gp * p1, 0.0)
    out = jnp.where(lane == 1, gp * p2, out)
    out = jnp.where(lane == 2, (l1 - N_GROUPS).astype(F32), out)
    out = jnp.where(lane == 3, (l2 - N_GROUPS).astype(F32), out)
    chosen = jnp.logical_or(lane == l1, lane == l2)
    onehot = jnp.where(chosen, 1.0, 0.0)
    ri = lax.broadcasted_iota(jnp.int32, (TM, TM), 0)
    ci = lax.broadcasted_iota(jnp.int32, (TM, TM), 1)
    earlier = jnp.dot(jnp.where(ri > ci, 1.0, 0.0).astype(BF16), onehot.astype(BF16), preferred_element_type=F32)
    before = earlier + cnt_scr[0:1, :]
    out = jnp.where(lane == 4, jnp.sum(jnp.where(lane == l1, before, 0.0), axis=-1, keepdims=True), out)
    out = jnp.where(lane == 5, jnp.sum(jnp.where(lane == l2, before, 0.0), axis=-1, keepdims=True), out)
    r_ref[...] = out
    cnt = cnt_scr[...] + jnp.sum(onehot, axis=0, keepdims=True)
    cnt_scr[...] = cnt
    cnt_ref[...] = cnt


def _router_call(x1, modrows, wr, br, mod_row):
    m, d = x1.shape
    nt = m // TM
    return pl.pallas_call(
        _router_kernel,
        grid=(nt,),
        in_specs=[pl.BlockSpec((TM, d), lambda t: (t, 0)), pl.BlockSpec((1, 6, d), lambda t: (mod_row(t), 0, 0)),
                  _const_spec(wr.shape), _const_spec(br.shape)],
        out_specs=[pl.BlockSpec((TM, ROW_TILES, LANE), lambda t: (t, 0, 0)), pl.BlockSpec((TM, LANE), lambda t: (t, 0)),
                   pl.BlockSpec((8, LANE), lambda t: (0, 0))],
        out_shape=[jax.ShapeDtypeStruct((m, ROW_TILES, LANE), F32), jax.ShapeDtypeStruct((m, LANE), F32),
                   jax.ShapeDtypeStruct((8, LANE), F32)],
        scratch_shapes=[pltpu.VMEM((8, LANE), F32)],
        compiler_params=_cparams(),
        name="router",
    )(x1, modrows, wr, br)


GATHER_UNROLL = 8


def _row_copy(src_hbm, dst, sem, row, j):
    return pltpu.make_async_copy(src_hbm.at[pl.ds(row, 1)], dst.at[pl.ds(j, 1)], sem)


def _start_gather(src_hbm, idx_ref, buf, sem, slot, n_rows):
    def body(jj, carry):
        for u in range(GATHER_UNROLL):
            j = jj * GATHER_UNROLL + u
            _row_copy(src_hbm, buf.at[slot], sem.at[slot], idx_ref[0, 0, j], j).start(priority=u % 2)
        return carry
    lax.fori_loop(0, n_rows // GATHER_UNROLL, body, 0)


def _wait_gather(src_hbm, buf, sem, slot, n_rows):
    def body(j, carry):
        _row_copy(src_hbm, buf.at[slot], sem.at[slot], 0, j).wait()
        return carry
    lax.fori_loop(0, n_rows, body, 0, unroll=8)


def _pipelined_gather(src_hbm, idx_ref, idx_next_ref, buf, sem, n_rows):
    i = pl.program_id(0)
    n = pl.num_programs(0)
    slot = i % 2

    @pl.when(i == 0)
    def _():
        _start_gather(src_hbm, idx_ref, buf, sem, 0, n_rows)

    @pl.when(i + 1 < n)
    def _():
        _start_gather(src_hbm, idx_next_ref, buf, sem, 1 - slot, n_rows)

    _wait_gather(src_hbm, buf, sem, slot, n_rows)
    return slot


def _expert_kernel(blk_exp_ref, tok_ref, tok_next_ref, h_hbm, w13_ref, w2_ref, o_ref, buf, sem):
    slot = _pipelined_gather(h_hbm, tok_ref, tok_next_ref, buf, sem, MOE_ROWS)
    x = jnp.concatenate(_from_token_tiles(buf[slot]), axis=-1).astype(BF16)
    y = jnp.dot(x, w13_ref[0], preferred_element_type=F32)
    act = (_silu(y[:, :D_EXPERT]) * y[:, D_EXPERT:]).astype(BF16)
    o_ref[...] = _to_token_tiles(jnp.dot(act, w2_ref[0], preferred_element_type=F32))


def _expert_call(blk_exp, tok_buf, h3, w13, w2):
    n_blk = blk_exp.shape[0]
    d = D_MODEL
    tok3 = tok_buf.reshape(n_blk, 1, MOE_ROWS)
    grid_spec = pltpu.PrefetchScalarGridSpec(
        num_scalar_prefetch=1,
        grid=(n_blk,),
        in_specs=[pl.BlockSpec((1, 1, MOE_ROWS), lambda i, be: (i, 0, 0), memory_space=pltpu.SMEM),
                  pl.BlockSpec((1, 1, MOE_ROWS), lambda i, be: (jnp.minimum(i + 1, n_blk - 1), 0, 0),
                               memory_space=pltpu.SMEM),
                  pl.BlockSpec(memory_space=pl.ANY),
                  pl.BlockSpec((1, d, 2 * D_EXPERT), lambda i, be: (be[i], 0, 0)),
                  pl.BlockSpec((1, D_EXPERT, d), lambda i, be: (be[i], 0, 0))],
        out_specs=pl.BlockSpec((MOE_ROWS, ROW_TILES, LANE), lambda i, be: (i, 0, 0)),
        scratch_shapes=[pltpu.VMEM((2, MOE_ROWS, ROW_TILES, LANE), F32), pltpu.SemaphoreType.DMA((2,))],
    )
    return pl.pallas_call(
        _expert_kernel,
        grid_spec=grid_spec,
        out_shape=jax.ShapeDtypeStruct((n_blk * MOE_ROWS, ROW_TILES, LANE), F32),
        compiler_params=_cparams(),
        name="expert_ffn",
    )(blk_exp, tok3, tok3, h3, w13, w2)


def _combine_kernel(pos_ref, pos_next_ref, y_hbm, x_ref, r_ref, mod_ref, par_ref, o_ref, buf, sem, *, alpha):
    slot = _pipelined_gather(y_hbm, pos_ref, pos_next_ref, buf, sem, 2 * TM)
    r = r_ref[...]
    g0 = jnp.broadcast_to(r[:, 0:1], (TM, LANE))
    g1 = jnp.broadcast_to(r[:, 1:2], (TM, LANE))
    y2 = jnp.concatenate([g0 * c[:TM] + g1 * c[TM:] for c in _from_token_tiles(buf[slot])], axis=-1)
    g2 = mod_ref[0, 5:6, :]
    o_ref[...] = _layernorm(alpha * x_ref[...] + g2 * y2, par_ref[0:1, :], par_ref[1:2, :])


def _combine_call(pos, yb, x1, route, modrows, par, alpha, mod_row):
    m, d = x1.shape
    nt = m // TM
    pos3 = pos.reshape(nt, 1, 2 * TM)

    def mod_map(t):
        return (mod_row(t), 0, 0)

    return pl.pallas_call(
        functools.partial(_combine_kernel, alpha=alpha),
        grid=(nt,),
        in_specs=[pl.BlockSpec((1, 1, 2 * TM), lambda t: (t, 0, 0), memory_space=pltpu.SMEM),
                  pl.BlockSpec((1, 1, 2 * TM), lambda t: (jnp.minimum(t + 1, nt - 1), 0, 0), memory_space=pltpu.SMEM),
                  pl.BlockSpec(memory_space=pl.ANY),
                  pl.BlockSpec((TM, d), lambda t: (t, 0)), pl.BlockSpec((TM, LANE), lambda t: (t, 0)),
                  pl.BlockSpec((1, 6, d), mod_map), _const_spec(par.shape)],
        out_specs=pl.BlockSpec((TM, d), lambda t: (t, 0)),
        out_shape=jax.ShapeDtypeStruct((m, d), F32),
        scratch_shapes=[pltpu.VMEM((2, 2 * TM, ROW_TILES, LANE), F32), pltpu.SemaphoreType.DMA((2,))],
        compiler_params=_cparams(),
        name="moe_combine",
    )(pos3, pos3, yb, x1, route, modrows, par)


def _rope_rot_perm():
    nf = QK_ROPE // 4
    src = np.zeros((QK_ROPE,), np.int32)
    sign = np.zeros((QK_ROPE,), np.float32)
    for a in range(2):
        for f in range(nf):
            i0 = a * 2 * nf + f
            i1 = a * 2 * nf + nf + f
            src[i0], sign[i0] = i1, -1.0
            src[i1], sign[i1] = i0, 1.0
    return src, sign


def _rope_tables(n_lat, n_ctx):
    nf = QK_ROPE // 4
    rows = n_lat // GRID_W
    inv = ROPE_BASE ** (-jnp.arange(nf, dtype=F32) / nf)
    r = jnp.repeat(jnp.arange(rows, dtype=F32), GRID_W)
    col = jnp.tile(jnp.arange(GRID_W, dtype=F32), rows)
    ang = jnp.stack([r[:, None] * inv, col[:, None] * inv], axis=1)
    cos = jnp.broadcast_to(jnp.cos(ang)[:, :, None, :], (n_lat, 2, 2, nf)).reshape(n_lat, QK_ROPE)
    sin = jnp.broadcast_to(jnp.sin(ang)[:, :, None, :], (n_lat, 2, 2, nf)).reshape(n_lat, QK_ROPE)
    cos = jnp.concatenate([jnp.ones((n_ctx, QK_ROPE), F32), cos], 0)
    sin = jnp.concatenate([jnp.zeros((n_ctx, QK_ROPE), F32), sin], 0)
    s = n_lat + n_ctx
    z = lambda w: jnp.zeros((s, w), F32)
    pad = LANE - QK_NOPE - QK_ROPE
    tqc = jnp.concatenate([jnp.full((s, QK_NOPE), MLA_SCALE, F32), cos * MLA_SCALE, z(pad)], 1)
    tqs = jnp.concatenate([z(QK_NOPE), sin * MLA_SCALE, z(pad)], 1)
    tkc = jnp.concatenate([z(QK_NOPE), cos, z(pad)], 1)
    tks = jnp.concatenate([z(QK_NOPE), sin, z(pad)], 1)
    return tqc, tqs, tkc, tks


def _head_pad_cols(w, width):
    k = w.shape[0]
    w = w.reshape(k, MLA_HEADS, width)
    return jnp.pad(w, ((0, 0), (0, 0), (0, HEAD_PAD - width))).reshape(k, MLA_HEADS * HEAD_PAD)


def _layer_weights(l, w_in, b_in, conv_a_w, conv_a_b, ln_a_g, ln_a_b, w_a_out, g_q, g_kv, w_uq, w_qr, w_uk, w_uv,
                   w_b_out, conv_c_w, a_log, dt_bias, g_o, w_c_out, w_out, ln1_g, ln1_b, w_rg, b_rg, w_re, b_re,
                   w1, w3, w2, ln2_g, ln2_b):
    src, sign = _rope_rot_perm()
    wi, bi = w_in[l], b_in[l][None, :]
    sizes = (2 * CONV_CH, Q_RANK, KV_RANK, QK_ROPE, 512, 512, 512, 512, 16, 16, 3 * D_MODEL)
    offs = np.concatenate([[0], np.cumsum(sizes)])

    def cols(a, i, j=None):
        return a[:, offs[i]:offs[(i if j is None else j) + 1]]

    def regroup(a):
        kr = cols(a, 3)
        z = lambda w: jnp.zeros((a.shape[0], w), a.dtype)
        kr_grp = jnp.concatenate([z(QK_NOPE), kr, z(32), z(QK_NOPE), kr[:, src] * sign, z(32)], 1)
        ba_grp = jnp.concatenate([cols(a, 8, 9), z(LANE - 32)], 1)
        return jnp.concatenate([cols(a, 0), cols(a, 1), cols(a, 2), kr_grp, cols(a, 4, 6), cols(a, 7), ba_grp,
                                cols(a, 10)], 1)

    w_all = regroup(wi).astype(BF16)
    b_all = regroup(bi)
    zl = jnp.zeros((LANE,), F32)
    gpar = jnp.stack([zl.at[16:32].set(a_log[l].reshape(-1)), zl.at[16:32].set(dt_bias[l].reshape(-1))], 0)
    gpar = jnp.concatenate([gpar, jnp.zeros((6, LANE), F32)], 0)

    wq = jnp.concatenate([w_uq[l].reshape(Q_RANK, MLA_HEADS, QK_NOPE), w_qr[l].reshape(Q_RANK, MLA_HEADS, QK_ROPE)], 2)
    wq = _head_pad_cols(wq.reshape(Q_RANK, -1), QK_NOPE + QK_ROPE).astype(BF16)
    wqr_rot = (w_qr[l].reshape(Q_RANK, MLA_HEADS, QK_ROPE)[:, :, src] * sign)
    wqr = jnp.concatenate([jnp.zeros((Q_RANK, MLA_HEADS, QK_NOPE), F32), wqr_rot], 2)
    wqr = _head_pad_cols(wqr.reshape(Q_RANK, -1), QK_NOPE + QK_ROPE).astype(BF16)
    wk = _head_pad_cols(w_uk[l], QK_NOPE).astype(BF16)
    wv = _head_pad_cols(w_uv[l], V_DIM).astype(BF16)
    vone = jnp.zeros((MLA_HEADS, HEAD_PAD), F32).at[:, V_DIM].set(1.0).reshape(1, -1)
    wb = jnp.pad(w_b_out[l].reshape(MLA_HEADS, V_DIM, D_MODEL), ((0, 0), (0, HEAD_PAD - V_DIM), (0, 0)))
    wb = wb.reshape(MLA_HEADS * HEAD_PAD, D_MODEL).astype(BF16)

    wr = jnp.concatenate([w_rg[l], w_re[l], jnp.zeros((D_MODEL, LANE - N_GROUPS - N_EXPERTS), F32)], 1)
    wr_hi = wr.astype(BF16)
    wr = jnp.stack([wr_hi, (wr - wr_hi.astype(F32)).astype(BF16)], 0)
    br = jnp.concatenate([b_rg[l], b_re[l], jnp.zeros((LANE - N_GROUPS - N_EXPERTS,), F32)])[None, :]
    return dict(
        w_all=w_all, b_all=b_all, gpar=gpar,
        conv_a_w=conv_a_w[l], conv_a_par=jnp.stack([conv_a_b[l], ln_a_g[l], ln_a_b[l]] + [jnp.zeros_like(ln_a_b[l])] * 5, 0),
        gq=g_q[l][None, :], gkv=g_kv[l][None, :], wq=wq, wqr=wqr, wk=wk, wv=wv, vone=vone,
        conv_c_w=conv_c_w[l],
        wa=w_a_out[l].astype(BF16), wb=wb, wc=w_c_out[l].astype(BF16), wo=w_out[l].astype(BF16),
        go=jnp.tile(g_o[l], GDN_HEADS)[None, :],
        ln1=jnp.stack([ln1_g[l], ln1_b[l]] + [jnp.zeros_like(ln1_g[l])] * 6, 0),
        ln2=jnp.stack([ln2_g[l], ln2_b[l]] + [jnp.zeros_like(ln2_g[l])] * 6, 0),
        wr=wr, br=br,
        w13=jnp.concatenate([w1[l], w3[l]], -1).astype(BF16), w2=w2[l].astype(BF16),
    )


def _routing_tables(route, cnt, n_tok):
    eid = route[:, 2:4].astype(jnp.int32)
    rank = route[:, 4:6].astype(jnp.int32)
    n_blk = 2 * n_tok // MOE_ROWS + N_EXPERTS
    sizes = cnt[0, N_GROUPS:N_GROUPS + N_EXPERTS].astype(jnp.int32)
    psizes = (sizes + MOE_ROWS - 1) // MOE_ROWS * MOE_ROWS
    pends = jnp.cumsum(psizes)
    pstarts = pends - psizes
    blk_start = jnp.arange(n_blk, dtype=jnp.int32) * MOE_ROWS
    blk_exp = jnp.minimum(jnp.sum((blk_start[:, None] >= pends[None, :]).astype(jnp.int32), axis=1), N_EXPERTS - 1)
    onehot = (eid[:, :, None] == jnp.arange(N_EXPERTS, dtype=jnp.int32)).astype(jnp.int32)
    pos = jnp.sum(onehot * pstarts, axis=-1) + rank
    tok = jnp.broadcast_to(jnp.arange(n_tok, dtype=jnp.int32)[:, None], (n_tok, 2))
    tok_buf = jnp.zeros((n_blk * MOE_ROWS,), jnp.int32).at[pos.reshape(-1)].add(tok.reshape(-1))
    return blk_exp, tok_buf, pos


def kernel(x, c, ctx, c_ctx, w_mod, b_mod, w_in, b_in, conv_a_w, conv_a_b, ln_a_g, ln_a_b, w_a_out, g_q, g_kv, w_uq,
           w_qr, w_uk, w_uv, w_b_out, conv_c_w, a_log, dt_bias, g_o, w_c_out, w_out, ln1_g, ln1_b, w_rg, b_rg, w_re,
           b_re, w1, w3, w2, ln2_g, ln2_b):
    n_batch, n_lat, d = x.shape
    n_ctx = ctx.shape[1]
    depth = w_mod.shape[0]
    alpha = (2 * depth) ** 0.25
    s_len = n_ctx + n_lat
    m = n_batch * s_len
    tpb = s_len // TM
    n_ctx_tiles = n_ctx // TM
    assert n_ctx % TM == 0 and n_lat % TM == 0 and n_ctx % CHUNK == 0 and s_len % n_ctx == 0

    xcat = jnp.concatenate([ctx, x], axis=1).reshape(m, d)
    cvec = jnp.concatenate([c, c_ctx[None, :], jnp.zeros((16 - n_batch - 1, d), F32)], 0)
    tabs = _rope_tables(n_lat, n_ctx)
    ones_blk = jnp.kron(jnp.eye(GDN_HEADS, dtype=F32), jnp.ones((GDN_DK, GDN_DK), F32)).astype(BF16)
    params = (w_in, b_in, conv_a_w, conv_a_b, ln_a_g, ln_a_b, w_a_out, g_q, g_kv, w_uq, w_qr, w_uk, w_uv, w_b_out,
              conv_c_w, a_log, dt_bias, g_o, w_c_out, w_out, ln1_g, ln1_b, w_rg, b_rg, w_re, b_re, w1, w3, w2,
              ln2_g, ln2_b)
    tq = TM

    for l in range(depth):
        p = _layer_weights(l, *params)
        mod = _mod_call(cvec, w_mod[l].astype(BF16), b_mod[l][None, :])
        modrows = mod[:n_batch + 1].reshape(n_batch + 1, 6, d)

        u, qd, kvd, kr, gqkv, gg, ba, mg = _inproj_call(xcat, modrows, p["w_all"], p["b_all"], p["gpar"], tpb,
                                                        n_ctx_tiles)
        za = _conv_a_call(u, p["conv_a_w"], p["conv_a_par"], tpb, n_ctx_tiles)
        qp, kp, vp = _mla_proj_call(qd, kvd, kr, tabs, p["gq"], p["gkv"], p["wq"], p["wqr"], p["wk"], p["wv"],
                                    p["vone"], tpb)
        last = l == depth - 1
        zb = _attn_call(qp, kp, vp, n_batch, s_len, n_ctx, n_lat, s_len, tq)
        if not last:
            zb_c = _attn_call(qp, kp, vp, n_batch, s_len, 0, n_ctx, n_ctx, n_ctx)
            hp = MLA_HEADS * HEAD_PAD
            zb = jnp.concatenate([zb_c.reshape(n_batch, n_ctx, hp), zb.reshape(n_batch, n_lat, hp)], 1).reshape(m, hp)
        qkvn = _gdn_prep_call(gqkv, p["conv_c_w"], ones_blk, tpb, n_ctx_tiles)
        bgt2 = ba[:, :32].reshape(m // CHUNK, CHUNK, 2, 2, N_PAIRS, 2).transpose(0, 2, 3, 4, 5, 1)
        bgt2 = bgt2.reshape(m // CHUNK, 16, LANE)
        gw, gu0, gkd, gqg, gpm, gcx = _gdn_local_call(qkvn, ba, bgt2)
        of, ob = _gdn_scan_call(gw, gu0, gkd, gqg, gpm, gcx, n_batch, tpb, n_ctx_tiles)
        x1 = _merge_call(za, zb, of, ob, gg, mg, xcat, modrows, p["wa"], p["wb"], p["wc"], p["wo"], ones_blk, p["ln1"],
                         p["go"], alpha, tpb, n_ctx_tiles, last)
        n_tok = x1.shape[0]
        _, _, mod_row = _tile_maps(n_batch, tpb, n_ctx_tiles, last)
        h3, route, cnt = _router_call(x1, modrows, p["wr"], p["br"], mod_row)
        blk_exp, tok_buf, pos = _routing_tables(route, cnt, n_tok)
        yb = _expert_call(blk_exp, tok_buf, h3, p["w13"], p["w2"])
        pos_t = pos.reshape(n_tok // TM, TM, 2).transpose(0, 2, 1).reshape(-1)
        xcat = _combine_call(pos_t, yb, x1, route, modrows, p["ln2"], alpha, mod_row)

    return xcat.reshape(n_batch, n_lat, d)
```

```python
import functools

import jax
import jax.numpy as jnp
import numpy as np
from jax import lax
from jax.experimental import pallas as pl
from jax.experimental.pallas import tpu as pltpu

F32 = jnp.float32
BF16 = jnp.bfloat16

D_MODEL = 1024
GRID_W = 64
EPS = 1e-6
CONV_CH = 512
CONV_K = 31
MLA_HEADS = 8
QK_NOPE = 64
QK_ROPE = 32
V_DIM = 64
Q_RANK = 384
KV_RANK = 256
ROPE_BASE = 10000.0
MLA_SCALE = (QK_NOPE + QK_ROPE) ** -0.5
GDN_HEADS = 8
GDN_DK = 64
GDN_DV = 64
SHORT_K = 5
CHUNK = 64
N_GROUPS = 4
EXP_PER_GROUP = 8
N_EXPERTS = N_GROUPS * EXP_PER_GROUP
D_EXPERT = 256

LANE = 128
TM = 256
HEAD_PAD = LANE
MOE_ROWS = 256
VMEM_LIMIT = 56 * 1024 * 1024

OFF_GLU = 0
OFF_QD = OFF_GLU + 2 * CONV_CH
OFF_KVD = OFF_QD + Q_RANK
OFF_KR = OFF_KVD + KV_RANK
OFF_GQKV = OFF_KR + 2 * LANE
OFF_GG = OFF_GQKV + 3 * GDN_HEADS * GDN_DK
OFF_BA = OFF_GG + GDN_HEADS * GDN_DV
OFF_MG = OFF_BA + LANE
P_ALL = OFF_MG + 3 * D_MODEL


def _cparams(n_grid=1):
    return pltpu.CompilerParams(dimension_semantics=("arbitrary",) * n_grid, vmem_limit_bytes=VMEM_LIMIT)


def _const_spec(shape):
    nd = len(shape)
    return pl.BlockSpec(shape, lambda *_: (0,) * nd, pipeline_mode=pl.Buffered(1))


def _bdot(a, b):
    return jnp.dot(a.astype(BF16), b.astype(BF16), preferred_element_type=F32)


def _hdot(a, b):
    return jnp.dot(a, b, precision=lax.Precision.HIGHEST, preferred_element_type=F32)


def _dot_nt(a, b):
    return lax.dot_general(a.astype(BF16), b.astype(BF16), (((1,), (1,)), ((), ())), preferred_element_type=F32)


def _dot_tn(a, b):
    return lax.dot_general(a.astype(BF16), b.astype(BF16), (((0,), (0,)), ((), ())), preferred_element_type=F32)


def _silu(x):
    return x * jax.nn.sigmoid(x)


def _group_sumsq(x, ones_blk):
    x2 = x * x
    hi = x2.astype(BF16)
    lo = (x2 - hi.astype(F32)).astype(BF16)
    return jnp.dot(hi, ones_blk, preferred_element_type=F32) + jnp.dot(lo, ones_blk, preferred_element_type=F32)


def _layernorm(x, g, b):
    mu = jnp.mean(x, -1, keepdims=True)
    xc = x - mu
    var = jnp.mean(xc * xc, -1, keepdims=True)
    return xc * lax.rsqrt(var + EPS) * g + b


def _mod_kernel(c_ref, w_ref, b_ref, o_ref):
    o_ref[...] = _bdot(_silu(c_ref[...]), w_ref[...]) + b_ref[...]


def _mod_call(cvec, w, b):
    rows, d = cvec.shape
    n = w.shape[1]
    tn = 1536
    return pl.pallas_call(
        _mod_kernel,
        grid=(n // tn,),
        in_specs=[pl.BlockSpec((rows, d), lambda j: (0, 0)),
                  pl.BlockSpec((d, tn), lambda j: (0, j)),
                  pl.BlockSpec((1, tn), lambda j: (0, j))],
        out_specs=pl.BlockSpec((rows, tn), lambda j: (0, j)),
        out_shape=jax.ShapeDtypeStruct((rows, n), F32),
        compiler_params=_cparams(),
        name="mod_vectors",
    )(cvec, w, b)


def _inproj_kernel(x_ref, mod_ref, w_ref, b_ref, gpar_ref,
                   u_ref, qd_ref, kvd_ref, kr_ref, gqkv_ref, gg_ref, ba_ref, mg_ref):
    x = x_ref[...]
    sh = mod_ref[0, 0:1, :]
    sc = mod_ref[0, 1:2, :]
    h = (x * (1.0 + sc) + sh).astype(BF16)

    def proj(off, n):
        return jnp.dot(h, w_ref[:, off:off + n], preferred_element_type=F32) + b_ref[:, off:off + n]

    y = proj(OFF_GLU, 2 * CONV_CH)
    u_ref[...] = y[:, :CONV_CH] * jax.nn.sigmoid(y[:, CONV_CH:])
    qd_ref[...] = proj(OFF_QD, Q_RANK)
    kvd_ref[...] = proj(OFF_KVD, KV_RANK)
    kr_ref[...] = proj(OFF_KR, 2 * LANE)
    gqkv_ref[...] = proj(OFF_GQKV, 3 * GDN_HEADS * GDN_DK)
    gg_ref[...] = proj(OFF_GG, GDN_HEADS * GDN_DV)
    y = proj(OFF_BA, LANE)
    lane = lax.broadcasted_iota(jnp.int32, y.shape, 1)
    z = y + gpar_ref[1:2, :]
    sp = jnp.maximum(z, 0.0) + jnp.log1p(jnp.exp(-jnp.abs(z)))
    ba_ref[...] = jnp.where(lane < 2 * GDN_HEADS, jax.nn.sigmoid(y), -jnp.exp(gpar_ref[0:1, :]) * sp)
    mg_ref[...] = jax.nn.sigmoid(proj(OFF_MG, 3 * D_MODEL)).astype(BF16)


def _inproj_call(xcat, modrows, w_all, b_all, gpar, tpb, n_ctx_tiles):
    m, d = xcat.shape
    nt = m // TM
    n_batch = modrows.shape[0] - 1

    def mod_map(t):
        return (jnp.where(t % tpb < n_ctx_tiles, n_batch, t // tpb), 0, 0)

    widths = (CONV_CH, Q_RANK, KV_RANK, 2 * LANE, 3 * GDN_HEADS * GDN_DK, GDN_HEADS * GDN_DV, LANE, 3 * D_MODEL)
    dtypes = (F32, F32, F32, F32, F32, F32, F32, BF16)
    return pl.pallas_call(
        _inproj_kernel,
        grid=(nt,),
        in_specs=[pl.BlockSpec((TM, d), lambda t: (t, 0)),
                  pl.BlockSpec((1, 6, d), mod_map),
                  _const_spec(w_all.shape), _const_spec(b_all.shape), _const_spec(gpar.shape)],
        out_specs=[pl.BlockSpec((TM, w), lambda t: (t, 0)) for w in widths],
        out_shape=[jax.ShapeDtypeStruct((m, w), dt) for w, dt in zip(widths, dtypes)],
        compiler_params=_cparams(),
        name="in_proj",
    )(xcat, modrows, w_all, b_all, gpar)


CONV_HALO = 16
CONV_SPAN_PAD = (CONV_HALO + CONV_K // 2) // 8 * 8


def _conv_a_kernel(top_ref, cur_ref, bot_ref, w_ref, par_ref, o_ref, win_ref, sh_ref, *, tpb, n_ctx_tiles):
    t = pl.program_id(0)
    r = t % tpb
    top_ok = jnp.logical_and(r != 0, r != n_ctx_tiles)
    bot_ok = jnp.logical_and(r != n_ctx_tiles - 1, r != tpb - 1)
    win_ref[0:CONV_HALO, :] = jnp.where(top_ok, top_ref[...], 0.0)
    win_ref[CONV_HALO:CONV_HALO + TM, :] = cur_ref[...]
    win_ref[CONV_HALO + TM:, :] = jnp.where(bot_ok, bot_ref[...], 0.0)
    span = TM + CONV_SPAN_PAD
    for b in range(1, 8):
        sh_ref[b - 1] = win_ref[pl.ds(b, span), :]
    acc = jnp.zeros((TM, CONV_CH), F32)
    base = CONV_HALO - CONV_K // 2
    for k in range(CONV_K):
        a8, b = (base + k) // 8 * 8, (base + k) % 8
        tap = win_ref[pl.ds(a8, TM), :] if b == 0 else sh_ref[b - 1, pl.ds(a8, TM), :]
        acc = acc + tap * w_ref[k:k + 1, :]
    y = _layernorm(acc + par_ref[0:1, :], par_ref[1:2, :], par_ref[2:3, :])
    o_ref[...] = _silu(y).astype(BF16)


def _conv_a_call(u, w, par, tpb, n_ctx_tiles):
    m, c = u.shape
    nt = m // TM
    hb = TM // CONV_HALO
    last = m // CONV_HALO - 1
    wp = jnp.zeros((32, c), F32).at[:CONV_K].set(w)
    return pl.pallas_call(
        functools.partial(_conv_a_kernel, tpb=tpb, n_ctx_tiles=n_ctx_tiles),
        grid=(nt,),
        in_specs=[pl.BlockSpec((CONV_HALO, c), lambda t: (jnp.maximum(t * hb - 1, 0), 0)),
                  pl.BlockSpec((TM, c), lambda t: (t, 0)),
                  pl.BlockSpec((CONV_HALO, c), lambda t: (jnp.minimum((t + 1) * hb, last), 0)),
                  _const_spec(wp.shape), _const_spec(par.shape)],
        out_specs=pl.BlockSpec((TM, c), lambda t: (t, 0)),
        out_shape=jax.ShapeDtypeStruct((m, c), BF16),
        scratch_shapes=[pltpu.VMEM((TM + 2 * CONV_HALO, c), F32), pltpu.VMEM((7, TM + CONV_SPAN_PAD, c), F32)],
        compiler_params=_cparams(),
        name="conv_branch",
    )(u, u, u, wp, par)


def _mla_proj_kernel(qd_ref, kvd_ref, kr_ref, tqc_ref, tqs_ref, tkc_ref, tks_ref, gq_ref, gkv_ref,
                     wq_ref, wqr_ref, wk_ref, wv_ref, vone_ref, q_ref, k_ref, v_ref):
    qd = qd_ref[...]
    cq = qd * lax.rsqrt(jnp.mean(qd * qd, -1, keepdims=True) + EPS) * gq_ref[...]
    qf = _bdot(cq, wq_ref[...])
    qr = _bdot(cq, wqr_ref[...])
    kvd = kvd_ref[...]
    ckv = kvd * lax.rsqrt(jnp.mean(kvd * kvd, -1, keepdims=True) + EPS) * gkv_ref[...]
    kn = _bdot(ckv, wk_ref[...])
    krf = kr_ref[:, :LANE] * tkc_ref[...] + kr_ref[:, LANE:] * tks_ref[...]
    tqc = tqc_ref[...]
    tqs = tqs_ref[...]
    for h in range(MLA_HEADS):
        sl = slice(h * HEAD_PAD, (h + 1) * HEAD_PAD)
        q_ref[:, sl] = (qf[:, sl] * tqc + qr[:, sl] * tqs).astype(BF16)
        k_ref[:, sl] = (kn[:, sl] + krf).astype(BF16)
    v_ref[...] = (_bdot(ckv, wv_ref[...]) + vone_ref[...]).astype(BF16)


def _mla_proj_call(qd, kvd, kr, tabs, gq, gkv, wq, wqr, wk, wv, vone, tpb, n_ctx_tiles):
    m = qd.shape[0]
    nt = m // TM
    hp = MLA_HEADS * HEAD_PAD
    lpb = tpb - n_ctx_tiles
    n_lat_tiles = (nt // tpb) * lpb

    def q_tile(t):
        b, r = t // tpb, t % tpb
        return jnp.where(r >= n_ctx_tiles, b * lpb + r - n_ctx_tiles, n_lat_tiles + b * n_ctx_tiles + r)

    tab_spec = pl.BlockSpec((TM, LANE), lambda t: (t % tpb, 0))
    row = lambda w: pl.BlockSpec((TM, w), lambda t: (t, 0))
    return pl.pallas_call(
        _mla_proj_kernel,
        grid=(nt,),
        in_specs=[row(Q_RANK), row(KV_RANK), row(2 * LANE), tab_spec, tab_spec, tab_spec, tab_spec,
                  _const_spec(gq.shape), _const_spec(gkv.shape), _const_spec(wq.shape), _const_spec(wqr.shape),
                  _const_spec(wk.shape), _const_spec(wv.shape), _const_spec(vone.shape)],
        out_specs=[pl.BlockSpec((TM, hp), lambda t: (q_tile(t), 0)), row(hp), row(hp)],
        out_shape=[jax.ShapeDtypeStruct((m, hp), BF16)] * 3,
        compiler_params=_cparams(),
        name="mla_proj",
    )(qd, kvd, kr, *tabs, gq, gkv, wq, wqr, wk, wv, vone)


def _attn_kernel(q_ref, k_ref, v_ref, o_ref):
    for h in range(MLA_HEADS):
        sl = slice(h * HEAD_PAD, (h + 1) * HEAD_PAD)
        s = lax.dot_general(q_ref[:, sl], k_ref[:, sl], (((1,), (1,)), ((), ())), preferred_element_type=F32)
        m = jnp.max(s, axis=-1, keepdims=True)
        p = jnp.exp(s - m).astype(BF16)
        o = jnp.dot(p, v_ref[:, sl], preferred_element_type=F32)
        denom = o[:, V_DIM:V_DIM + 1]
        o_ref[:, sl] = (o * (1.0 / denom)).astype(BF16)


def _attn_call(q, k, v, n_batch, s_len, q_row0, q_len, k_len, tq):
    hp = MLA_HEADS * HEAD_PAD
    nq = q_len // tq
    q0 = q_row0 // tq
    kb = s_len // k_len
    return pl.pallas_call(
        _attn_kernel,
        grid=(n_batch, nq),
        in_specs=[pl.BlockSpec((tq, hp), lambda b, i: (q0 + b * nq + i, 0)),
                  pl.BlockSpec((k_len, hp), lambda b, i: (b * kb, 0), pipeline_mode=pl.Buffered(1)),
                  pl.BlockSpec((k_len, hp), lambda b, i: (b * kb, 0), pipeline_mode=pl.Buffered(1))],
        out_specs=pl.BlockSpec((tq, hp), lambda b, i: (b * nq + i, 0)),
        out_shape=jax.ShapeDtypeStruct((n_batch * q_len, hp), BF16),
        compiler_params=_cparams(2),
        name="attention",
    )(q, k, v)


GDN_HALO = 8


def _gdn_prep_kernel(top_ref, cur_ref, bot_ref, w_ref, ones_ref, o_ref, win_ref, *, tpb, n_ctx_tiles):
    t = pl.program_id(0)
    r = t % tpb
    top_ok = jnp.logical_and(r != 0, r != n_ctx_tiles)
    bot_ok = jnp.logical_and(r != n_ctx_tiles - 1, r != tpb - 1)
    win_ref[0:GDN_HALO, :] = jnp.where(top_ok, top_ref[...], 0.0)
    win_ref[GDN_HALO:GDN_HALO + TM, :] = cur_ref[...]
    win_ref[GDN_HALO + TM:, :] = jnp.where(bot_ok, bot_ref[...], 0.0)
    c = 3 * GDN_HEADS * GDN_DK
    acc = jnp.zeros((TM, c), F32)
    base = GDN_HALO - SHORT_K // 2
    for k in range(SHORT_K):
        acc = acc + win_ref[pl.ds(base + k, TM), :] * w_ref[k:k + 1, :]
    u = _silu(acc)
    qk = GDN_HEADS * GDN_DK
    q = u[:, :qk]
    kk = u[:, qk:2 * qk]
    ones_blk = ones_ref[...]
    q = q * lax.rsqrt(_group_sumsq(q, ones_blk) + EPS) * (GDN_DK ** -0.5)
    kk = kk * lax.rsqrt(_group_sumsq(kk, ones_blk) + EPS)
    o_ref[:, :qk] = q.astype(BF16)
    o_ref[:, qk:2 * qk] = kk.astype(BF16)
    o_ref[:, 2 * qk:] = u[:, 2 * qk:].astype(BF16)


def _gdn_prep_call(gqkv, w, ones_blk, tpb, n_ctx_tiles):
    m, c = gqkv.shape
    nt = m // TM
    hb = TM // GDN_HALO
    last = m // GDN_HALO - 1
    wp = jnp.zeros((8, c), F32).at[:SHORT_K].set(w)
    return pl.pallas_call(
        functools.partial(_gdn_prep_kernel, tpb=tpb, n_ctx_tiles=n_ctx_tiles),
        grid=(nt,),
        in_specs=[pl.BlockSpec((GDN_HALO, c), lambda t: (jnp.maximum(t * hb - 1, 0), 0)),
                  pl.BlockSpec((TM, c), lambda t: (t, 0)),
                  pl.BlockSpec((GDN_HALO, c), lambda t: (jnp.minimum((t + 1) * hb, last), 0)),
                  _const_spec(wp.shape), _const_spec(ones_blk.shape)],
        out_specs=pl.BlockSpec((TM, c), lambda t: (t, 0)),
        out_shape=jax.ShapeDtypeStruct((m, c), BF16),
        scratch_shapes=[pltpu.VMEM((TM + 2 * GDN_HALO, c), F32)],
        compiler_params=_cparams(),
        name="gdn_prep",
    )(gqkv, gqkv, gqkv, wp, ones_blk)


N_PAIRS = GDN_HEADS // 2
CPT = TM // CHUNK


def _left_half(shape):
    return lax.broadcasted_iota(jnp.int32, shape, len(shape) - 1) < GDN_DK


def _blockdiag(x2):
    left = _left_half(x2.shape)
    zero = jnp.zeros_like(x2)
    return jnp.concatenate([jnp.where(left, x2, zero), jnp.where(left, zero, x2)], axis=0).astype(BF16)


def _blockdiag_pair(x):
    return jnp.concatenate([_blockdiag(x[:, :LANE]), _blockdiag(x[:, LANE:])], axis=1)


def _hi_lo(x):
    hi = x.astype(BF16)
    return hi, (x - hi.astype(F32)).astype(BF16)


def _lane_bcast_pair(x, c0):
    shape = (x.shape[0], LANE)
    return jnp.where(_left_half(shape), jnp.broadcast_to(x[:, c0:c0 + 1], shape),
                     jnp.broadcast_to(x[:, c0 + 1:c0 + 2], shape))


def _gdn_local_kernel(qkv_ref, bg_ref, bgt_ref, w_ref, u0_ref, kd_ref, qg_ref, pm_ref, gcx_ref):
    qk = GDN_HEADS * GDN_DK
    nh2 = 2 * GDN_HEADS
    ri = lax.broadcasted_iota(jnp.int32, (CHUNK, LANE), 0)
    ci = lax.broadcasted_iota(jnp.int32, (CHUNK, LANE), 1) % CHUNK
    incl = (ri >= ci, ri <= ci)
    strict = (ri > ci, ri < ci)
    eye2 = (ri == ci).astype(F32)
    r64 = lax.broadcasted_iota(jnp.int32, (CHUNK, CHUNK), 0)
    c64 = lax.broadcasted_iota(jnp.int32, (CHUNK, CHUNK), 1)
    m_incl = ((r64 >= c64).astype(F32), (r64 <= c64).astype(F32))
    rj = lax.broadcasted_iota(jnp.int32, (LANE, LANE), 0)
    cl = lax.broadcasted_iota(jnp.int32, (LANE, LANE), 1)
    same = (rj // CHUNK) == (cl // CHUNK)
    m_incl_t = (jnp.logical_and(same, rj % CHUNK <= cl % CHUNK).astype(F32),
                jnp.logical_and(same, rj % CHUNK >= cl % CHUNK).astype(F32))
    ones_bd = same.astype(F32)

    chains = []
    for j in range(CPT):
        rows = slice(j * CHUNK, (j + 1) * CHUNK)
        bg = bg_ref[rows, :]
        bgt = bgt_ref[j]
        gam = (_hdot(m_incl[0], bg), _hdot(m_incl[1], bg))
        gam_t = (_hdot(bgt, m_incl_t[0]), _hdot(bgt, m_incl_t[1]))
        gt_rows = _hdot(bgt, ones_bd)
        gcx_ref[j * 8:(j + 1) * 8, :] = jnp.exp(gt_rows[8:16])
        for p in range(N_PAIRS):
            ls = slice(p * LANE, (p + 1) * LANE)
            q2 = qkv_ref[rows, p * LANE:(p + 1) * LANE]
            k2 = qkv_ref[rows, qk + p * LANE:qk + (p + 1) * LANE]
            v2 = qkv_ref[rows, 2 * qk + p * LANE:2 * qk + (p + 1) * LANE]
            left = _left_half(k2.shape)
            zero = jnp.zeros_like(k2)
            k_bd = jnp.concatenate([jnp.where(left, k2, zero), jnp.where(left, zero, k2)], axis=0)
            kq = lax.dot_general(jnp.concatenate([k2, q2], axis=0), k_bd, (((1,), (1,)), ((), ())),
                                 preferred_element_type=F32)
            kk2, qk2 = kq[:CHUNK], kq[CHUNK:]
            k2f, q2f, v2f = k2.astype(F32), q2.astype(F32), v2.astype(F32)
            for d in range(2):
                c0 = d * GDN_HEADS + 2 * p
                r = 8 + d * N_PAIRS + p
                b2 = _lane_bcast_pair(bg, c0)
                gcol2 = _lane_bcast_pair(gam[d], nh2 + c0)
                grow2 = gam_t[d][r:r + 1, :]
                gt2 = gt_rows[r:r + 1, :]
                decay2 = jnp.where(incl[d], jnp.exp(jnp.where(incl[d], gcol2 - grow2, 0.0)), 0.0)
                n2 = -jnp.where(strict[d], b2 * decay2 * kk2, 0.0)
                eg2 = jnp.exp(gcol2)
                rhs = jnp.concatenate([(b2 * eg2) * k2f, b2 * v2f], axis=1)
                kd_ref[d, rows, ls] = (k2f * jnp.exp(gt2 - gcol2)).astype(BF16)
                qg_ref[d, rows, ls] = (q2f * eg2).astype(BF16)
                pm_ref[d, rows, ls] = (decay2 * qk2).astype(BF16)
                chains.append(dict(d=d, rows=rows, ls=ls, n=n2, pw=n2, t=eye2 + n2, rhs=rhs))

    for ch in chains:
        ch["pw"] = jnp.dot(ch["pw"].astype(BF16), _blockdiag(ch["pw"]), preferred_element_type=F32)
    for _ in range(4):
        for ch in chains:
            res = jnp.dot(jnp.concatenate([ch["pw"], ch["t"]], axis=0).astype(BF16), _blockdiag(ch["pw"]),
                          preferred_element_type=F32)
            ch["pw"] = res[:CHUNK]
            ch["t"] = ch["t"] + res[CHUNK:]
    for ch in chains:
        ch["t"] = ch["t"] + jnp.dot(ch["t"].astype(BF16), _blockdiag(ch["pw"]), preferred_element_type=F32)
    for ch in chains:
        ch["tb"] = ch["t"].astype(BF16)
        ch["x"] = jnp.dot(ch["tb"], _blockdiag_pair(ch["rhs"]), preferred_element_type=F32)
    for ch in chains:
        n_hi, n_lo = _hi_lo(ch["n"])
        x_hi, x_lo = _hi_lo(ch["x"])
        top = jnp.dot(jnp.concatenate([n_hi, n_lo], axis=0), _blockdiag_pair(x_hi), preferred_element_type=F32)
        nx = top[:CHUNK] + top[CHUNK:] + jnp.dot(n_hi, _blockdiag_pair(x_lo), preferred_element_type=F32)
        ch["r"] = ch["rhs"] - ch["x"] + nx
    for ch in chains:
        x = ch["x"] + jnp.dot(ch["tb"], _blockdiag_pair(ch["r"]), preferred_element_type=F32)
        w_ref[ch["d"], ch["rows"], ch["ls"]] = x[:, :LANE].astype(BF16)
        u0_ref[ch["d"], ch["rows"], ch["ls"]] = x[:, LANE:]


def _gdn_local_call(qkvn, bg, bgt2):
    m, c = qkvn.shape
    nt = m // TM
    dv = GDN_HEADS * GDN_DV
    spec2 = pl.BlockSpec((2, TM, dv), lambda t: (0, t, 0))
    shp = lambda dt: jax.ShapeDtypeStruct((2, m, dv), dt)
    return pl.pallas_call(
        _gdn_local_kernel,
        grid=(nt,),
        in_specs=[pl.BlockSpec((TM, c), lambda t: (t, 0)), pl.BlockSpec((TM, LANE), lambda t: (t, 0)),
                  pl.BlockSpec((CPT, 16, LANE), lambda t: (t, 0, 0))],
        out_specs=[spec2, spec2, spec2, spec2, spec2, pl.BlockSpec((CPT * 8, LANE), lambda t: (t, 0))],
        out_shape=[shp(BF16), shp(F32), shp(BF16), shp(BF16), shp(BF16),
                   jax.ShapeDtypeStruct((m // CHUNK * 8, LANE), F32)],
        compiler_params=_cparams(),
        name="gdn_local",
    )(qkvn, bg, bgt2)


def _gdn_scan_kernel(wf, u0f, kdf, qgf, pmf, gcf, wb, u0b, kdb, qgb, pmb, gcb, of_ref, ob_ref, s_ref):
    @pl.when(pl.program_id(1) == 0)
    def _():
        s_ref[...] = jnp.zeros_like(s_ref)

    refs = ((wf, u0f, kdf, qgf, pmf, gcf, of_ref), (wb, u0b, kdb, qgb, pmb, gcb, ob_ref))
    left = _left_half((CHUNK, LANE))
    state = [s_ref[i] for i in range(2 * N_PAIRS)]
    for sub in range(CPT):
        cur = []
        for d in range(2):
            j = sub if d == 0 else CPT - 1 - sub
            rows = slice(j * CHUNK, (j + 1) * CHUNK)
            for p in range(N_PAIRS):
                cur.append((d, p, rows, slice(p * LANE, (p + 1) * LANE), j * 8 + d * N_PAIRS + p))
        res = []
        for d, p, rows, ls, gr in cur:
            w_r, _, _, qg_r, _, _, _ = refs[d]
            lhs = jnp.concatenate([w_r[rows, ls], qg_r[rows, ls]], axis=0)
            res.append(jnp.dot(lhs, _blockdiag(state[d * N_PAIRS + p]), preferred_element_type=F32))
        us = []
        for (d, p, rows, ls, gr), r in zip(cur, res):
            us.append(refs[d][1][rows, ls] - r[:CHUNK])
        for (d, p, rows, ls, gr), r, u in zip(cur, res, us):
            _, _, kd_r, _, pm_r, gc_r, o_r = refs[d]
            o_r[rows, ls] = r[CHUNK:] + jnp.dot(pm_r[rows, ls], _blockdiag(u), preferred_element_type=F32)
            upd = lax.dot_general(kd_r[rows, ls], u.astype(BF16), (((0,), (0,)), ((), ())),
                                  preferred_element_type=F32)
            i = d * N_PAIRS + p
            state[i] = gc_r[gr:gr + 1, :] * state[i] + jnp.where(left, upd[:CHUNK], upd[CHUNK:])
    for i in range(2 * N_PAIRS):
        s_ref[i] = state[i]


def _gdn_scan_call(w, u0, kd, qg, pm, gcx, n_batch, tpb, n_ctx_tiles):
    m, dv = w.shape[1], w.shape[2]

    def fwd_tile(b, j):
        return b * tpb + j

    def bwd_tile(b, j):
        return b * tpb + jnp.where(j < n_ctx_tiles, n_ctx_tiles - 1 - j, tpb - 1 + n_ctx_tiles - j)

    def specs(d, tile):
        big = pl.BlockSpec((None, TM, dv), lambda b, j: (d, tile(b, j), 0))
        return [big, big, big, big, big, pl.BlockSpec((CPT * 8, LANE), lambda b, j: (tile(b, j), 0))]

    return pl.pallas_call(
        _gdn_scan_kernel,
        grid=(n_batch, tpb),
        in_specs=specs(0, fwd_tile) + specs(1, bwd_tile),
        out_specs=[pl.BlockSpec((TM, dv), lambda b, j: (fwd_tile(b, j), 0)),
                   pl.BlockSpec((TM, dv), lambda b, j: (bwd_tile(b, j), 0))],
        out_shape=[jax.ShapeDtypeStruct((m, dv), F32)] * 2,
        scratch_shapes=[pltpu.VMEM((2 * N_PAIRS, GDN_DK, LANE), F32)],
        compiler_params=_cparams(2),
        name="gdn_scan",
    )(w, u0, kd, qg, pm, gcx, w, u0, kd, qg, pm, gcx)


def _merge_kernel(za_ref, zb_ref, of_ref, ob_ref, gg_ref, mg_ref, x_ref, mod_ref, wa_ref, wb_ref, wc_ref, wo_ref,
                  ones_ref, par_ref, go_ref, o_ref, *, alpha):
    oc = of_ref[...] + ob_ref[...]
    ms = _group_sumsq(oc, ones_ref[...]) * (1.0 / GDN_DV)
    zc = oc * lax.rsqrt(ms + EPS) * go_ref[...] * _silu(gg_ref[...])
    ya = jnp.dot(za_ref[...], wa_ref[...], preferred_element_type=F32)
    yb = jnp.dot(zb_ref[...], wb_ref[...], preferred_element_type=F32)
    yc = _bdot(zc, wc_ref[...])
    d = D_MODEL
    mix = (mg_ref[:, 0:d].astype(F32) * ya + mg_ref[:, d:2 * d].astype(F32) * yb
           + mg_ref[:, 2 * d:3 * d].astype(F32) * yc)
    y = _bdot(mix, wo_ref[...])
    g1 = mod_ref[0, 2:3, :]
    o_ref[...] = _layernorm(alpha * x_ref[...] + g1 * y, par_ref[0:1, :], par_ref[1:2, :])


def _tile_maps(n_batch, tpb, n_ctx_tiles, lat_only):
    if lat_only:
        lpb = tpb - n_ctx_tiles
        return n_batch * lpb, (lambda t: (t // lpb) * tpb + n_ctx_tiles + t % lpb), (lambda t: t // lpb)
    return n_batch * tpb, (lambda t: t), (lambda t: jnp.where(t % tpb < n_ctx_tiles, n_batch, t // tpb))


def _merge_call(za, zb, of, ob, gg, mg, xcat, modrows, wa, wb, wc, wo, ones_blk, par, go, alpha, tpb, n_ctx_tiles,
                lat_only):
    d = xcat.shape[1]
    nt, src_tile, mod_row = _tile_maps(modrows.shape[0] - 1, tpb, n_ctx_tiles, lat_only)
    row = lambda w: pl.BlockSpec((TM, w), lambda t: (src_tile(t), 0))
    own = lambda w: pl.BlockSpec((TM, w), lambda t: (t, 0))
    return pl.pallas_call(
        functools.partial(_merge_kernel, alpha=alpha),
        grid=(nt,),
        in_specs=[row(za.shape[1]), own(zb.shape[1]), row(of.shape[1]), row(ob.shape[1]), row(gg.shape[1]),
                  row(mg.shape[1]), row(d),
                  pl.BlockSpec((1, 6, d), lambda t: (mod_row(t), 0, 0)),
                  _const_spec(wa.shape), _const_spec(wb.shape), _const_spec(wc.shape), _const_spec(wo.shape),
                  _const_spec(ones_blk.shape), _const_spec(par.shape), _const_spec(go.shape)],
        out_specs=own(d),
        out_shape=jax.ShapeDtypeStruct((nt * TM, d), F32),
        compiler_params=_cparams(),
        name="merge",
    )(za, zb, of, ob, gg, mg, xcat, modrows, wa, wb, wc, wo, ones_blk, par, go)


ROW_TILES = D_MODEL // LANE


def _to_token_tiles(x):
    chunks = jnp.stack([x[:, s * LANE:(s + 1) * LANE] for s in range(ROW_TILES)], axis=0)
    return pltpu.einshape("stl->tsl", chunks)


def _from_token_tiles(x3):
    chunks = pltpu.einshape("tsl->stl", x3)
    return [chunks[s] for s in range(ROW_TILES)]


def _router_kernel(x_ref, mod_ref, w_ref, b_ref, h_ref, r_ref, cnt_ref, cnt_scr):
    @pl.when(pl.program_id(0) == 0)
    def _():
        cnt_scr[...] = jnp.zeros_like(cnt_scr)

    sh = mod_ref[0, 3:4, :]
    sc = mod_ref[0, 4:5, :]
    h = x_ref[...] * (1.0 + sc) + sh
    h_ref[...] = _to_token_tiles(h)
    h_hi, h_lo = _hi_lo(h)
    lg = (jnp.dot(h_hi, w_ref[0], preferred_element_type=F32) + jnp.dot(h_lo, w_ref[0], preferred_element_type=F32)
          + jnp.dot(h_hi, w_ref[1], preferred_element_type=F32)) + b_ref[...]
    lane = lax.broadcasted_iota(jnp.int32, lg.shape, 1)
    neg = jnp.float32(-1e30)
    big = jnp.int32(1 << 20)
    is_g = lane < N_GROUPS
    gmax = jnp.max(jnp.where(is_g, lg, neg), axis=-1, keepdims=True)
    gsum = jnp.sum(jnp.where(is_g, jnp.exp(lg - gmax), 0.0), axis=-1, keepdims=True)
    gp = 1.0 / gsum
    gi = jnp.min(jnp.where(jnp.logical_and(is_g, lg == gmax), lane, big), axis=-1, keepdims=True)
    lo = N_GROUPS + gi * EXP_PER_GROUP
    in_grp = jnp.logical_and(lane >= lo, lane < lo + EXP_PER_GROUP)
    v1 = jnp.max(jnp.where(in_grp, lg, neg), axis=-1, keepdims=True)
    l1 = jnp.min(jnp.where(jnp.logical_and(in_grp, lg == v1), lane, big), axis=-1, keepdims=True)
    rest = jnp.logical_and(in_grp, lane != l1)
    v2 = jnp.max(jnp.where(rest, lg, neg), axis=-1, keepdims=True)
    l2 = jnp.min(jnp.where(jnp.logical_and(rest, lg == v2), lane, big), axis=-1, keepdims=True)
    e2 = jnp.exp(v2 - v1)
    p1 = 1.0 / (1.0 + e2)
    p2 = e2 / (1.0 + e2)
    out = jnp.where(lane == 0, gp * p1, 0.0)
    out = jnp.where(lane == 1, gp * p2, out)
    out = jnp.where(lane == 2, (l1 - N_GROUPS).astype(F32), out)
    out = jnp.where(lane == 3, (l2 - N_GROUPS).astype(F32), out)
    chosen = jnp.logical_or(lane == l1, lane == l2)
    onehot = jnp.where(chosen, 1.0, 0.0)
    ri = lax.broadcasted_iota(jnp.int32, (TM, TM), 0)
    ci = lax.broadcasted_iota(jnp.int32, (TM, TM), 1)
    earlier = jnp.dot(jnp.where(ri > ci, 1.0, 0.0).astype(BF16), onehot.astype(BF16), preferred_element_type=F32)
    before = earlier + cnt_scr[0:1, :]
    out = jnp.where(lane == 4, jnp.sum(jnp.where(lane == l1, before, 0.0), axis=-1, keepdims=True), out)
    out = jnp.where(lane == 5, jnp.sum(jnp.where(lane == l2, before, 0.0), axis=-1, keepdims=True), out)
    r_ref[...] = out
    cnt = cnt_scr[...] + jnp.sum(onehot, axis=0, keepdims=True)
    cnt_scr[...] = cnt
    cnt_ref[...] = cnt


def _router_call(x1, modrows, wr, br, mod_row):
    m, d = x1.shape
    nt = m // TM
    return pl.pallas_call(
        _router_kernel,
        grid=(nt,),
        in_specs=[pl.BlockSpec((TM, d), lambda t: (t, 0)), pl.BlockSpec((1, 6, d), lambda t: (mod_row(t), 0, 0)),
                  _const_spec(wr.shape), _const_spec(br.shape)],
        out_specs=[pl.BlockSpec((TM, ROW_TILES, LANE), lambda t: (t, 0, 0)), pl.BlockSpec((TM, LANE), lambda t: (t, 0)),
                   pl.BlockSpec((8, LANE), lambda t: (0, 0))],
        out_shape=[jax.ShapeDtypeStruct((m, ROW_TILES, LANE), F32), jax.ShapeDtypeStruct((m, LANE), F32),
                   jax.ShapeDtypeStruct((8, LANE), F32)],
        scratch_shapes=[pltpu.VMEM((8, LANE), F32)],
        compiler_params=_cparams(),
        name="router",
    )(x1, modrows, wr, br)


GATHER_UNROLL = 16


def _row_copy(src_hbm, dst, sem, row, j):
    return pltpu.make_async_copy(src_hbm.at[pl.ds(row, 1)], dst.at[pl.ds(j, 1)], sem)


def _start_gather(src_hbm, idx_ref, buf, sem, slot, n_rows):
    def body(jj, carry):
        for u in range(GATHER_UNROLL):
            j = jj * GATHER_UNROLL + u
            _row_copy(src_hbm, buf.at[slot], sem.at[slot], idx_ref[0, 0, j], j).start(priority=u % 2)
        return carry
    lax.fori_loop(0, n_rows // GATHER_UNROLL, body, 0)


def _wait_gather(src_hbm, buf, sem, slot, n_rows):
    def body(j, carry):
        _row_copy(src_hbm, buf.at[slot], sem.at[slot], 0, j).wait()
        return carry
    lax.fori_loop(0, n_rows, body, 0, unroll=8)


def _pipelined_gather(src_hbm, idx_ref, idx_next_ref, buf, sem, n_rows):
    i = pl.program_id(0)
    n = pl.num_programs(0)
    slot = i % 2

    @pl.when(i == 0)
    def _():
        _start_gather(src_hbm, idx_ref, buf, sem, 0, n_rows)

    @pl.when(i + 1 < n)
    def _():
        _start_gather(src_hbm, idx_next_ref, buf, sem, 1 - slot, n_rows)

    _wait_gather(src_hbm, buf, sem, slot, n_rows)
    return slot


def _expert_kernel(blk_exp_ref, tok_ref, tok_next_ref, h_hbm, w1_ref, w3_ref, w2_ref, o_ref, buf, sem):
    slot = _pipelined_gather(h_hbm, tok_ref, tok_next_ref, buf, sem, MOE_ROWS)
    x = jnp.concatenate(_from_token_tiles(buf[slot]), axis=-1).astype(BF16)
    act = (_silu(_bdot(x, w1_ref[0])) * _bdot(x, w3_ref[0])).astype(BF16)
    o_ref[...] = _to_token_tiles(_bdot(act, w2_ref[0]))


def _expert_call(blk_exp, tok_buf, h3, w1, w3, w2, l):
    n_blk = blk_exp.shape[0]
    d = D_MODEL
    tok3 = tok_buf.reshape(n_blk, 1, MOE_ROWS)
    grid_spec = pltpu.PrefetchScalarGridSpec(
        num_scalar_prefetch=1,
        grid=(n_blk,),
        in_specs=[pl.BlockSpec((1, 1, MOE_ROWS), lambda i, be: (i, 0, 0), memory_space=pltpu.SMEM),
                  pl.BlockSpec((1, 1, MOE_ROWS), lambda i, be: (jnp.minimum(i + 1, n_blk - 1), 0, 0),
                               memory_space=pltpu.SMEM),
                  pl.BlockSpec(memory_space=pl.ANY),
                  pl.BlockSpec((None, 1, d, D_EXPERT), lambda i, be: (l, be[i], 0, 0)),
                  pl.BlockSpec((None, 1, d, D_EXPERT), lambda i, be: (l, be[i], 0, 0)),
                  pl.BlockSpec((None, 1, D_EXPERT, d), lambda i, be: (l, be[i], 0, 0))],
        out_specs=pl.BlockSpec((MOE_ROWS, ROW_TILES, LANE), lambda i, be: (i, 0, 0)),
        scratch_shapes=[pltpu.VMEM((2, MOE_ROWS, ROW_TILES, LANE), F32), pltpu.SemaphoreType.DMA((2,))],
    )
    return pl.pallas_call(
        _expert_kernel,
        grid_spec=grid_spec,
        out_shape=jax.ShapeDtypeStruct((n_blk * MOE_ROWS, ROW_TILES, LANE), F32),
        compiler_params=_cparams(),
        name="expert_ffn",
    )(blk_exp, tok3, tok3, h3, w1, w3, w2)


def _combine_kernel(pos_ref, pos_next_ref, y_hbm, x_ref, r_ref, mod_ref, par_ref, o_ref, buf, sem, *, alpha):
    slot = _pipelined_gather(y_hbm, pos_ref, pos_next_ref, buf, sem, 2 * TM)
    r = r_ref[...]
    g0 = jnp.broadcast_to(r[:, 0:1], (TM, LANE))
    g1 = jnp.broadcast_to(r[:, 1:2], (TM, LANE))
    y2 = jnp.concatenate([g0 * c[:TM] + g1 * c[TM:] for c in _from_token_tiles(buf[slot])], axis=-1)
    g2 = mod_ref[0, 5:6, :]
    o_ref[...] = _layernorm(alpha * x_ref[...] + g2 * y2, par_ref[0:1, :], par_ref[1:2, :])


def _combine_call(pos, yb, x1, route, modrows, par, alpha, mod_row):
    m, d = x1.shape
    nt = m // TM
    pos3 = pos.reshape(nt, 1, 2 * TM)

    def mod_map(t):
        return (mod_row(t), 0, 0)

    return pl.pallas_call(
        functools.partial(_combine_kernel, alpha=alpha),
        grid=(nt,),
        in_specs=[pl.BlockSpec((1, 1, 2 * TM), lambda t: (t, 0, 0), memory_space=pltpu.SMEM),
                  pl.BlockSpec((1, 1, 2 * TM), lambda t: (jnp.minimum(t + 1, nt - 1), 0, 0), memory_space=pltpu.SMEM),
                  pl.BlockSpec(memory_space=pl.ANY),
                  pl.BlockSpec((TM, d), lambda t: (t, 0)), pl.BlockSpec((TM, LANE), lambda t: (t, 0)),
                  pl.BlockSpec((1, 6, d), mod_map), _const_spec(par.shape)],
        out_specs=pl.BlockSpec((TM, d), lambda t: (t, 0)),
        out_shape=jax.ShapeDtypeStruct((m, d), F32),
        scratch_shapes=[pltpu.VMEM((2, 2 * TM, ROW_TILES, LANE), F32), pltpu.SemaphoreType.DMA((2,))],
        compiler_params=_cparams(),
        name="moe_combine",
    )(pos3, pos3, yb, x1, route, modrows, par)


def _rope_rot_perm():
    nf = QK_ROPE // 4
    src = np.zeros((QK_ROPE,), np.int32)
    sign = np.zeros((QK_ROPE,), np.float32)
    for a in range(2):
        for f in range(nf):
            i0 = a * 2 * nf + f
            i1 = a * 2 * nf + nf + f
            src[i0], sign[i0] = i1, -1.0
            src[i1], sign[i1] = i0, 1.0
    return src, sign


def _rope_tables(n_lat, n_ctx):
    nf = QK_ROPE // 4
    rows = n_lat // GRID_W
    inv = ROPE_BASE ** (-jnp.arange(nf, dtype=F32) / nf)
    r = jnp.repeat(jnp.arange(rows, dtype=F32), GRID_W)
    col = jnp.tile(jnp.arange(GRID_W, dtype=F32), rows)
    ang = jnp.stack([r[:, None] * inv, col[:, None] * inv], axis=1)
    cos = jnp.broadcast_to(jnp.cos(ang)[:, :, None, :], (n_lat, 2, 2, nf)).reshape(n_lat, QK_ROPE)
    sin = jnp.broadcast_to(jnp.sin(ang)[:, :, None, :], (n_lat, 2, 2, nf)).reshape(n_lat, QK_ROPE)
    cos = jnp.concatenate([jnp.ones((n_ctx, QK_ROPE), F32), cos], 0)
    sin = jnp.concatenate([jnp.zeros((n_ctx, QK_ROPE), F32), sin], 0)
    s = n_lat + n_ctx
    z = lambda w: jnp.zeros((s, w), F32)
    pad = LANE - QK_NOPE - QK_ROPE
    tqc = jnp.concatenate([jnp.full((s, QK_NOPE), MLA_SCALE, F32), cos * MLA_SCALE, z(pad)], 1)
    tqs = jnp.concatenate([z(QK_NOPE), sin * MLA_SCALE, z(pad)], 1)
    tkc = jnp.concatenate([z(QK_NOPE), cos, z(pad)], 1)
    tks = jnp.concatenate([z(QK_NOPE), sin, z(pad)], 1)
    return tqc, tqs, tkc, tks


def _head_pad_cols(w, width):
    k = w.shape[0]
    w = w.reshape(k, MLA_HEADS, width)
    return jnp.pad(w, ((0, 0), (0, 0), (0, HEAD_PAD - width))).reshape(k, MLA_HEADS * HEAD_PAD)


def _layer_weights(l, w_in, b_in, conv_a_w, conv_a_b, ln_a_g, ln_a_b, w_a_out, g_q, g_kv, w_uq, w_qr, w_uk, w_uv,
                   w_b_out, conv_c_w, a_log, dt_bias, g_o, w_c_out, w_out, ln1_g, ln1_b, w_rg, b_rg, w_re, b_re,
                   w1, w3, w2, ln2_g, ln2_b):
    src, sign = _rope_rot_perm()
    wi, bi = w_in[l], b_in[l][None, :]
    sizes = (2 * CONV_CH, Q_RANK, KV_RANK, QK_ROPE, 512, 512, 512, 512, 16, 16, 3 * D_MODEL)
    offs = np.concatenate([[0], np.cumsum(sizes)])

    def cols(a, i, j=None):
        return a[:, offs[i]:offs[(i if j is None else j) + 1]]

    def regroup(a):
        kr = cols(a, 3)
        z = lambda w: jnp.zeros((a.shape[0], w), a.dtype)
        kr_grp = jnp.concatenate([z(QK_NOPE), kr, z(32), z(QK_NOPE), kr[:, src] * sign, z(32)], 1)
        ba_grp = jnp.concatenate([cols(a, 8, 9), z(LANE - 32)], 1)
        return jnp.concatenate([cols(a, 0), cols(a, 1), cols(a, 2), kr_grp, cols(a, 4, 6), cols(a, 7), ba_grp,
                                cols(a, 10)], 1)

    w_all = regroup(wi).astype(BF16)
    b_all = regroup(bi)
    zl = jnp.zeros((LANE,), F32)
    gpar = jnp.stack([zl.at[16:32].set(a_log[l].reshape(-1)), zl.at[16:32].set(dt_bias[l].reshape(-1))], 0)
    gpar = jnp.concatenate([gpar, jnp.zeros((6, LANE), F32)], 0)

    wq = jnp.concatenate([w_uq[l].reshape(Q_RANK, MLA_HEADS, QK_NOPE), w_qr[l].reshape(Q_RANK, MLA_HEADS, QK_ROPE)], 2)
    wq = _head_pad_cols(wq.reshape(Q_RANK, -1), QK_NOPE + QK_ROPE).astype(BF16)
    wqr_rot = (w_qr[l].reshape(Q_RANK, MLA_HEADS, QK_ROPE)[:, :, src] * sign)
    wqr = jnp.concatenate([jnp.zeros((Q_RANK, MLA_HEADS, QK_NOPE), F32), wqr_rot], 2)
    wqr = _head_pad_cols(wqr.reshape(Q_RANK, -1), QK_NOPE + QK_ROPE).astype(BF16)
    wk = _head_pad_cols(w_uk[l], QK_NOPE).astype(BF16)
    wv = _head_pad_cols(w_uv[l], V_DIM).astype(BF16)
    vone = jnp.zeros((MLA_HEADS, HEAD_PAD), F32).at[:, V_DIM].set(1.0).reshape(1, -1)
    wb = jnp.pad(w_b_out[l].reshape(MLA_HEADS, V_DIM, D_MODEL), ((0, 0), (0, HEAD_PAD - V_DIM), (0, 0)))
    wb = wb.reshape(MLA_HEADS * HEAD_PAD, D_MODEL).astype(BF16)

    wr = jnp.concatenate([w_rg[l], w_re[l], jnp.zeros((D_MODEL, LANE - N_GROUPS - N_EXPERTS), F32)], 1)
    wr_hi = wr.astype(BF16)
    wr = jnp.stack([wr_hi, (wr - wr_hi.astype(F32)).astype(BF16)], 0)
    br = jnp.concatenate([b_rg[l], b_re[l], jnp.zeros((LANE - N_GROUPS - N_EXPERTS,), F32)])[None, :]
    return dict(
        w_all=w_all, b_all=b_all, gpar=gpar,
        conv_a_w=conv_a_w[l], conv_a_par=jnp.stack([conv_a_b[l], ln_a_g[l], ln_a_b[l]] + [jnp.zeros_like(ln_a_b[l])] * 5, 0),
        gq=g_q[l][None, :], gkv=g_kv[l][None, :], wq=wq, wqr=wqr, wk=wk, wv=wv, vone=vone,
        conv_c_w=conv_c_w[l],
        wa=w_a_out[l].astype(BF16), wb=wb, wc=w_c_out[l].astype(BF16), wo=w_out[l].astype(BF16),
        go=jnp.tile(g_o[l], GDN_HEADS)[None, :],
        ln1=jnp.stack([ln1_g[l], ln1_b[l]] + [jnp.zeros_like(ln1_g[l])] * 6, 0),
        ln2=jnp.stack([ln2_g[l], ln2_b[l]] + [jnp.zeros_like(ln2_g[l])] * 6, 0),
        wr=wr, br=br,
    )


def _routing_tables(route, cnt, n_tok):
    eid = route[:, 2:4].astype(jnp.int32)
    rank = route[:, 4:6].astype(jnp.int32)
    n_blk = 2 * n_tok // MOE_ROWS + N_EXPERTS
    sizes = cnt[0, N_GROUPS:N_GROUPS + N_EXPERTS].astype(jnp.int32)
    psizes = (sizes + MOE_ROWS - 1) // MOE_ROWS * MOE_ROWS
    pends = jnp.cumsum(psizes)
    pstarts = pends - psizes
    blk_start = jnp.arange(n_blk, dtype=jnp.int32) * MOE_ROWS
    blk_exp = jnp.minimum(jnp.sum((blk_start[:, None] >= pends[None, :]).astype(jnp.int32), axis=1), N_EXPERTS - 1)
    onehot = (eid[:, :, None] == jnp.arange(N_EXPERTS, dtype=jnp.int32)).astype(jnp.int32)
    pos = jnp.sum(onehot * pstarts, axis=-1) + rank
    tok = jnp.broadcast_to(jnp.arange(n_tok, dtype=jnp.int32)[:, None], (n_tok, 2))
    tok_buf = jnp.zeros((n_blk * MOE_ROWS,), jnp.int32).at[pos.reshape(-1)].add(tok.reshape(-1))
    return blk_exp, tok_buf, pos


def kernel(x, c, ctx, c_ctx, w_mod, b_mod, w_in, b_in, conv_a_w, conv_a_b, ln_a_g, ln_a_b, w_a_out, g_q, g_kv, w_uq,
           w_qr, w_uk, w_uv, w_b_out, conv_c_w, a_log, dt_bias, g_o, w_c_out, w_out, ln1_g, ln1_b, w_rg, b_rg, w_re,
           b_re, w1, w3, w2, ln2_g, ln2_b):
    n_batch, n_lat, d = x.shape
    n_ctx = ctx.shape[1]
    depth = w_mod.shape[0]
    alpha = (2 * depth) ** 0.25
    s_len = n_ctx + n_lat
    m = n_batch * s_len
    tpb = s_len // TM
    n_ctx_tiles = n_ctx // TM
    assert n_ctx % TM == 0 and n_lat % TM == 0 and n_ctx % CHUNK == 0 and s_len % n_ctx == 0

    xcat = jnp.concatenate([ctx, x], axis=1).reshape(m, d)
    cvec = jnp.concatenate([c, c_ctx[None, :], jnp.zeros((16 - n_batch - 1, d), F32)], 0)
    tabs = _rope_tables(n_lat, n_ctx)
    ones_blk = jnp.kron(jnp.eye(GDN_HEADS, dtype=F32), jnp.ones((GDN_DK, GDN_DK), F32)).astype(BF16)
    params = (w_in, b_in, conv_a_w, conv_a_b, ln_a_g, ln_a_b, w_a_out, g_q, g_kv, w_uq, w_qr, w_uk, w_uv, w_b_out,
              conv_c_w, a_log, dt_bias, g_o, w_c_out, w_out, ln1_g, ln1_b, w_rg, b_rg, w_re, b_re, w1, w3, w2,
              ln2_g, ln2_b)
    tq = 512 if n_lat % 512 == 0 else TM

    for l in range(depth):
        p = _layer_weights(l, *params)
        mod = _mod_call(cvec, w_mod[l].astype(BF16), b_mod[l][None, :])
        modrows = mod[:n_batch + 1].reshape(n_batch + 1, 6, d)

        u, qd, kvd, kr, gqkv, gg, ba, mg = _inproj_call(xcat, modrows, p["w_all"], p["b_all"], p["gpar"], tpb,
                                                        n_ctx_tiles)
        za = _conv_a_call(u, p["conv_a_w"], p["conv_a_par"], tpb, n_ctx_tiles)
        qp, kp, vp = _mla_proj_call(qd, kvd, kr, tabs, p["gq"], p["gkv"], p["wq"], p["wqr"], p["wk"], p["wv"],
                                    p["vone"], tpb, n_ctx_tiles)
        last = l == depth - 1
        zb = _attn_call(qp, kp, vp, n_batch, s_len, 0, n_lat, s_len, tq)
        if not last:
            zb_c = _attn_call(qp, kp, vp, n_batch, s_len, n_batch * n_lat, n_ctx, n_ctx, n_ctx)
            hp = MLA_HEADS * HEAD_PAD
            zb = jnp.concatenate([zb_c.reshape(n_batch, n_ctx, hp), zb.reshape(n_batch, n_lat, hp)], 1).reshape(m, hp)
        qkvn = _gdn_prep_call(gqkv, p["conv_c_w"], ones_blk, tpb, n_ctx_tiles)
        bgt2 = ba[:, :32].reshape(m // CHUNK, CHUNK, 2, 2, N_PAIRS, 2).transpose(0, 2, 3, 4, 5, 1)
        bgt2 = bgt2.reshape(m // CHUNK, 16, LANE)
        gw, gu0, gkd, gqg, gpm, gcx = _gdn_local_call(qkvn, ba, bgt2)
        of, ob = _gdn_scan_call(gw, gu0, gkd, gqg, gpm, gcx, n_batch, tpb, n_ctx_tiles)
        x1 = _merge_call(za, zb, of, ob, gg, mg, xcat, modrows, p["wa"], p["wb"], p["wc"], p["wo"], ones_blk, p["ln1"],
                         p["go"], alpha, tpb, n_ctx_tiles, last)
        n_tok = x1.shape[0]
        _, _, mod_row = _tile_maps(n_batch, tpb, n_ctx_tiles, last)
        h3, route, cnt = _router_call(x1, modrows, p["wr"], p["br"], mod_row)
        blk_exp, tok_buf, pos = _routing_tables(route, cnt, n_tok)
        yb = _expert_call(blk_exp, tok_buf, h3, w1, w3, w2, l)
        pos_t = pos.reshape(n_tok // TM, TM, 2).transpose(0, 2, 1).reshape(-1)
        xcat = _combine_call(pos_t, yb, x1, route, modrows, p["ln2"], alpha, mod_row)

    return xcat.reshape(n_batch, n_lat, d)
```

```python
import functools

import jax
import jax.numpy as jnp
import numpy as np
from jax import lax
from jax.experimental import pallas as pl
from jax.experimental.pallas import tpu as pltpu

F32 = jnp.float32
BF16 = jnp.bfloat16

D_MODEL = 1024
GRID_W = 64
EPS = 1e-6
CONV_CH = 512
CONV_K = 31
MLA_HEADS = 8
QK_NOPE = 64
QK_ROPE = 32
V_DIM = 64
Q_RANK = 384
KV_RANK = 256
ROPE_BASE = 10000.0
MLA_SCALE = (QK_NOPE + QK_ROPE) ** -0.5
LOG2_E = 1.4426950408889634
GDN_HEADS = 8
GDN_DK = 64
GDN_DV = 64
SHORT_K = 5
CHUNK = 64
N_GROUPS = 4
EXP_PER_GROUP = 8
N_EXPERTS = N_GROUPS * EXP_PER_GROUP
D_EXPERT = 256

LANE = 128
TM = 256
HEAD_PAD = LANE
MOE_ROWS = 256
VMEM_LIMIT = 56 * 1024 * 1024

OFF_GLU = 0
OFF_QD = OFF_GLU + 2 * CONV_CH
OFF_KVD = OFF_QD + Q_RANK
OFF_KR = OFF_KVD + KV_RANK
OFF_GQKV = OFF_KR + 2 * LANE
OFF_GG = OFF_GQKV + 3 * GDN_HEADS * GDN_DK
OFF_BA = OFF_GG + GDN_HEADS * GDN_DV
OFF_MG = OFF_BA + LANE
P_ALL = OFF_MG + 3 * D_MODEL


def _cparams(n_grid=1):
    return pltpu.CompilerParams(dimension_semantics=("arbitrary",) * n_grid, vmem_limit_bytes=VMEM_LIMIT)


def _const_spec(shape):
    nd = len(shape)
    return pl.BlockSpec(shape, lambda *_: (0,) * nd, pipeline_mode=pl.Buffered(1))


def _bdot(a, b):
    return jnp.dot(a.astype(BF16), b.astype(BF16), preferred_element_type=F32)


def _hdot(a, b):
    return jnp.dot(a, b, precision=lax.Precision.HIGHEST, preferred_element_type=F32)


def _dot_nt(a, b):
    return lax.dot_general(a.astype(BF16), b.astype(BF16), (((1,), (1,)), ((), ())), preferred_element_type=F32)


def _dot_tn(a, b):
    return lax.dot_general(a.astype(BF16), b.astype(BF16), (((0,), (0,)), ((), ())), preferred_element_type=F32)


def _silu(x):
    return x * jax.nn.sigmoid(x)


def _group_sumsq(x, ones_blk):
    x2 = x * x
    hi = x2.astype(BF16)
    lo = (x2 - hi.astype(F32)).astype(BF16)
    return jnp.dot(hi, ones_blk, preferred_element_type=F32) + jnp.dot(lo, ones_blk, preferred_element_type=F32)


def _layernorm(x, g, b):
    mu = jnp.mean(x, -1, keepdims=True)
    xc = x - mu
    var = jnp.mean(xc * xc, -1, keepdims=True)
    return xc * lax.rsqrt(var + EPS) * g + b


def _mod_kernel(c_ref, w_ref, b_ref, o_ref):
    o_ref[...] = _bdot(_silu(c_ref[...]), w_ref[...]) + b_ref[...]


def _mod_call(cvec, w, b):
    rows, d = cvec.shape
    n = w.shape[1]
    tn = 1536
    return pl.pallas_call(
        _mod_kernel,
        grid=(n // tn,),
        in_specs=[pl.BlockSpec((rows, d), lambda j: (0, 0)),
                  pl.BlockSpec((d, tn), lambda j: (0, j)),
                  pl.BlockSpec((1, tn), lambda j: (0, j))],
        out_specs=pl.BlockSpec((rows, tn), lambda j: (0, j)),
        out_shape=jax.ShapeDtypeStruct((rows, n), F32),
        compiler_params=_cparams(),
        name="mod_vectors",
    )(cvec, w, b)


def _inproj_kernel(x_ref, mod_ref, w_ref, b_ref, gpar_ref,
                   u_ref, qd_ref, kvd_ref, kr_ref, gqkv_ref, gg_ref, ba_ref, mg_ref):
    x = x_ref[...]
    sh = mod_ref[0, 0:1, :]
    sc = mod_ref[0, 1:2, :]
    h = (x * (1.0 + sc) + sh).astype(BF16)

    def proj(off, n):
        return jnp.dot(h, w_ref[:, off:off + n], preferred_element_type=F32) + b_ref[:, off:off + n]

    y = proj(OFF_GLU, 2 * CONV_CH)
    u_ref[...] = y[:, :CONV_CH] * jax.nn.sigmoid(y[:, CONV_CH:])
    qd_ref[...] = proj(OFF_QD, Q_RANK)
    kvd_ref[...] = proj(OFF_KVD, KV_RANK)
    kr_ref[...] = proj(OFF_KR, 2 * LANE)
    gqkv_ref[...] = proj(OFF_GQKV, 3 * GDN_HEADS * GDN_DK)
    gg_ref[...] = proj(OFF_GG, GDN_HEADS * GDN_DV)
    y = proj(OFF_BA, LANE)
    lane = lax.broadcasted_iota(jnp.int32, y.shape, 1)
    z = y + gpar_ref[1:2, :]
    sp = jnp.maximum(z, 0.0) + jnp.log1p(jnp.exp(-jnp.abs(z)))
    ba_ref[...] = jnp.where(lane < 2 * GDN_HEADS, jax.nn.sigmoid(y), -jnp.exp(gpar_ref[0:1, :]) * sp)
    mg_ref[...] = jax.nn.sigmoid(proj(OFF_MG, 3 * D_MODEL)).astype(BF16)


def _inproj_call(xcat, modrows, w_all, b_all, gpar, tpb, n_ctx_tiles):
    m, d = xcat.shape
    nt = m // TM
    n_batch = modrows.shape[0] - 1

    def mod_map(t):
        return (jnp.where(t % tpb < n_ctx_tiles, n_batch, t // tpb), 0, 0)

    widths = (CONV_CH, Q_RANK, KV_RANK, 2 * LANE, 3 * GDN_HEADS * GDN_DK, GDN_HEADS * GDN_DV, LANE, 3 * D_MODEL)
    dtypes = (F32, F32, F32, F32, F32, F32, F32, BF16)
    return pl.pallas_call(
        _inproj_kernel,
        grid=(nt,),
        in_specs=[pl.BlockSpec((TM, d), lambda t: (t, 0)),
                  pl.BlockSpec((1, 6, d), mod_map),
                  _const_spec(w_all.shape), _const_spec(b_all.shape), _const_spec(gpar.shape)],
        out_specs=[pl.BlockSpec((TM, w), lambda t: (t, 0)) for w in widths],
        out_shape=[jax.ShapeDtypeStruct((m, w), dt) for w, dt in zip(widths, dtypes)],
        compiler_params=_cparams(),
        name="in_proj",
    )(xcat, modrows, w_all, b_all, gpar)


CONV_HALO = 16
CONV_SPAN_PAD = (CONV_HALO + CONV_K // 2) // 8 * 8


def _conv_a_kernel(top_ref, cur_ref, bot_ref, w_ref, par_ref, o_ref, win_ref, sh_ref, *, tpb, n_ctx_tiles):
    t = pl.program_id(0)
    r = t % tpb
    top_ok = jnp.logical_and(r != 0, r != n_ctx_tiles)
    bot_ok = jnp.logical_and(r != n_ctx_tiles - 1, r != tpb - 1)
    win_ref[0:CONV_HALO, :] = jnp.where(top_ok, top_ref[...], 0.0)
    win_ref[CONV_HALO:CONV_HALO + TM, :] = cur_ref[...]
    win_ref[CONV_HALO + TM:, :] = jnp.where(bot_ok, bot_ref[...], 0.0)
    span = TM + CONV_SPAN_PAD
    for b in range(1, 8):
        sh_ref[b - 1] = win_ref[pl.ds(b, span), :]
    acc = jnp.zeros((TM, CONV_CH), F32)
    base = CONV_HALO - CONV_K // 2
    for k in range(CONV_K):
        a8, b = (base + k) // 8 * 8, (base + k) % 8
        tap = win_ref[pl.ds(a8, TM), :] if b == 0 else sh_ref[b - 1, pl.ds(a8, TM), :]
        acc = acc + tap * w_ref[k:k + 1, :]
    y = _layernorm(acc + par_ref[0:1, :], par_ref[1:2, :], par_ref[2:3, :])
    o_ref[...] = _silu(y).astype(BF16)


def _conv_a_call(u, w, par, tpb, n_ctx_tiles):
    m, c = u.shape
    nt = m // TM
    hb = TM // CONV_HALO
    last = m // CONV_HALO - 1
    wp = jnp.zeros((32, c), F32).at[:CONV_K].set(w)
    return pl.pallas_call(
        functools.partial(_conv_a_kernel, tpb=tpb, n_ctx_tiles=n_ctx_tiles),
        grid=(nt,),
        in_specs=[pl.BlockSpec((CONV_HALO, c), lambda t: (jnp.maximum(t * hb - 1, 0), 0)),
                  pl.BlockSpec((TM, c), lambda t: (t, 0)),
                  pl.BlockSpec((CONV_HALO, c), lambda t: (jnp.minimum((t + 1) * hb, last), 0)),
                  _const_spec(wp.shape), _const_spec(par.shape)],
        out_specs=pl.BlockSpec((TM, c), lambda t: (t, 0)),
        out_shape=jax.ShapeDtypeStruct((m, c), BF16),
        scratch_shapes=[pltpu.VMEM((TM + 2 * CONV_HALO, c), F32), pltpu.VMEM((7, TM + CONV_SPAN_PAD, c), F32)],
        compiler_params=_cparams(),
        name="conv_branch",
    )(u, u, u, wp, par)


def _mla_proj_kernel(qd_ref, kvd_ref, kr_ref, tqc_ref, tqs_ref, tkc_ref, tks_ref, gq_ref, gkv_ref,
                     wq_ref, wqr_ref, wk_ref, wv_ref, vone_ref, q_ref, k_ref, v_ref):
    qd = qd_ref[...]
    cq = qd * lax.rsqrt(jnp.mean(qd * qd, -1, keepdims=True) + EPS) * gq_ref[...]
    qf = _bdot(cq, wq_ref[...])
    qr = _bdot(cq, wqr_ref[...])
    kvd = kvd_ref[...]
    ckv = kvd * lax.rsqrt(jnp.mean(kvd * kvd, -1, keepdims=True) + EPS) * gkv_ref[...]
    kn = _bdot(ckv, wk_ref[...])
    krf = kr_ref[:, :LANE] * tkc_ref[...] + kr_ref[:, LANE:] * tks_ref[...]
    tqc = tqc_ref[...]
    tqs = tqs_ref[...]
    for h in range(MLA_HEADS):
        sl = slice(h * HEAD_PAD, (h + 1) * HEAD_PAD)
        q_ref[:, sl] = (qf[:, sl] * tqc + qr[:, sl] * tqs).astype(BF16)
        k_ref[:, sl] = (kn[:, sl] + krf).astype(BF16)
    v_ref[...] = (_bdot(ckv, wv_ref[...]) + vone_ref[...]).astype(BF16)


def _mla_proj_call(qd, kvd, kr, tabs, gq, gkv, wq, wqr, wk, wv, vone, tpb, n_ctx_tiles):
    m = qd.shape[0]
    nt = m // TM
    hp = MLA_HEADS * HEAD_PAD
    lpb = tpb - n_ctx_tiles
    n_lat_tiles = (nt // tpb) * lpb

    def q_tile(t):
        b, r = t // tpb, t % tpb
        return jnp.where(r >= n_ctx_tiles, b * lpb + r - n_ctx_tiles, n_lat_tiles + b * n_ctx_tiles + r)

    tab_spec = pl.BlockSpec((TM, LANE), lambda t: (t % tpb, 0))
    row = lambda w: pl.BlockSpec((TM, w), lambda t: (t, 0))
    return pl.pallas_call(
        _mla_proj_kernel,
        grid=(nt,),
        in_specs=[row(Q_RANK), row(KV_RANK), row(2 * LANE), tab_spec, tab_spec, tab_spec, tab_spec,
                  _const_spec(gq.shape), _const_spec(gkv.shape), _const_spec(wq.shape), _const_spec(wqr.shape),
                  _const_spec(wk.shape), _const_spec(wv.shape), _const_spec(vone.shape)],
        out_specs=[pl.BlockSpec((TM, hp), lambda t: (q_tile(t), 0)), row(hp), row(hp)],
        out_shape=[jax.ShapeDtypeStruct((m, hp), BF16)] * 3,
        compiler_params=_cparams(),
        name="mla_proj",
    )(qd, kvd, kr, *tabs, gq, gkv, wq, wqr, wk, wv, vone)


def _attn_kernel(q_ref, k_ref, v_ref, o_ref):
    for h in range(MLA_HEADS):
        sl = slice(h * HEAD_PAD, (h + 1) * HEAD_PAD)
        s = lax.dot_general(q_ref[:, sl], k_ref[:, sl], (((1,), (1,)), ((), ())), preferred_element_type=F32)
        m = jnp.max(s, axis=-1, keepdims=True)
        p = jnp.exp2(s - m).astype(BF16)
        o = jnp.dot(p, v_ref[:, sl], preferred_element_type=F32)
        denom = o[:, V_DIM:V_DIM + 1]
        o_ref[:, sl] = (o * (1.0 / denom)).astype(BF16)


def _attn_call(q, k, v, n_batch, s_len, q_row0, q_len, k_len, tq):
    hp = MLA_HEADS * HEAD_PAD
    nq = q_len // tq
    q0 = q_row0 // tq
    kb = s_len // k_len
    return pl.pallas_call(
        _attn_kernel,
        grid=(n_batch, nq),
        in_specs=[pl.BlockSpec((tq, hp), lambda b, i: (q0 + b * nq + i, 0)),
                  pl.BlockSpec((k_len, hp), lambda b, i: (b * kb, 0), pipeline_mode=pl.Buffered(1)),
                  pl.BlockSpec((k_len, hp), lambda b, i: (b * kb, 0), pipeline_mode=pl.Buffered(1))],
        out_specs=pl.BlockSpec((tq, hp), lambda b, i: (b * nq + i, 0)),
        out_shape=jax.ShapeDtypeStruct((n_batch * q_len, hp), BF16),
        compiler_params=_cparams(2),
        name="attention",
    )(q, k, v)


GDN_HALO = 8


def _gdn_prep_kernel(top_ref, cur_ref, bot_ref, w_ref, ones_ref, o_ref, win_ref, *, tpb, n_ctx_tiles):
    t = pl.program_id(0)
    r = t % tpb
    top_ok = jnp.logical_and(r != 0, r != n_ctx_tiles)
    bot_ok = jnp.logical_and(r != n_ctx_tiles - 1, r != tpb - 1)
    win_ref[0:GDN_HALO, :] = jnp.where(top_ok, top_ref[...], 0.0)
    win_ref[GDN_HALO:GDN_HALO + TM, :] = cur_ref[...]
    win_ref[GDN_HALO + TM:, :] = jnp.where(bot_ok, bot_ref[...], 0.0)
    c = 3 * GDN_HEADS * GDN_DK
    acc = jnp.zeros((TM, c), F32)
    base = GDN_HALO - SHORT_K // 2
    for k in range(SHORT_K):
        acc = acc + win_ref[pl.ds(base + k, TM), :] * w_ref[k:k + 1, :]
    u = _silu(acc)
    qk = GDN_HEADS * GDN_DK
    q = u[:, :qk]
    kk = u[:, qk:2 * qk]
    ones_blk = ones_ref[...]
    q = q * lax.rsqrt(_group_sumsq(q, ones_blk) + EPS) * (GDN_DK ** -0.5)
    kk = kk * lax.rsqrt(_group_sumsq(kk, ones_blk) + EPS)
    o_ref[:, :qk] = q.astype(BF16)
    o_ref[:, qk:2 * qk] = kk.astype(BF16)
    o_ref[:, 2 * qk:] = u[:, 2 * qk:].astype(BF16)


def _gdn_prep_call(gqkv, w, ones_blk, tpb, n_ctx_tiles):
    m, c = gqkv.shape
    nt = m // TM
    hb = TM // GDN_HALO
    last = m // GDN_HALO - 1
    wp = jnp.zeros((8, c), F32).at[:SHORT_K].set(w)
    return pl.pallas_call(
        functools.partial(_gdn_prep_kernel, tpb=tpb, n_ctx_tiles=n_ctx_tiles),
        grid=(nt,),
        in_specs=[pl.BlockSpec((GDN_HALO, c), lambda t: (jnp.maximum(t * hb - 1, 0), 0)),
                  pl.BlockSpec((TM, c), lambda t: (t, 0)),
                  pl.BlockSpec((GDN_HALO, c), lambda t: (jnp.minimum((t + 1) * hb, last), 0)),
                  _const_spec(wp.shape), _const_spec(ones_blk.shape)],
        out_specs=pl.BlockSpec((TM, c), lambda t: (t, 0)),
        out_shape=jax.ShapeDtypeStruct((m, c), BF16),
        scratch_shapes=[pltpu.VMEM((TM + 2 * GDN_HALO, c), F32)],
        compiler_params=_cparams(),
        name="gdn_prep",
    )(gqkv, gqkv, gqkv, wp, ones_blk)


N_PAIRS = GDN_HEADS // 2
CPT = TM // CHUNK


def _left_half(shape):
    return lax.broadcasted_iota(jnp.int32, shape, len(shape) - 1) < GDN_DK


def _blockdiag(x2):
    left = _left_half(x2.shape)
    zero = jnp.zeros_like(x2)
    return jnp.concatenate([jnp.where(left, x2, zero), jnp.where(left, zero, x2)], axis=0).astype(BF16)


def _blockdiag_pair(x):
    return jnp.concatenate([_blockdiag(x[:, :LANE]), _blockdiag(x[:, LANE:])], axis=1)


def _hi_lo(x):
    hi = x.astype(BF16)
    return hi, (x - hi.astype(F32)).astype(BF16)


def _lane_bcast_pair(x, c0):
    shape = (x.shape[0], LANE)
    return jnp.where(_left_half(shape), jnp.broadcast_to(x[:, c0:c0 + 1], shape),
                     jnp.broadcast_to(x[:, c0 + 1:c0 + 2], shape))


def _gdn_local_kernel(qkv_ref, bg_ref, bgt_ref, w_ref, u0_ref, kd_ref, qg_ref, pm_ref, gcx_ref):
    qk = GDN_HEADS * GDN_DK
    nh2 = 2 * GDN_HEADS
    ri = lax.broadcasted_iota(jnp.int32, (CHUNK, LANE), 0)
    ci = lax.broadcasted_iota(jnp.int32, (CHUNK, LANE), 1) % CHUNK
    incl = (ri >= ci, ri <= ci)
    strict = (ri > ci, ri < ci)
    eye2 = (ri == ci).astype(F32)
    r64 = lax.broadcasted_iota(jnp.int32, (CHUNK, CHUNK), 0)
    c64 = lax.broadcasted_iota(jnp.int32, (CHUNK, CHUNK), 1)
    m_incl = ((r64 >= c64).astype(F32), (r64 <= c64).astype(F32))
    rj = lax.broadcasted_iota(jnp.int32, (LANE, LANE), 0)
    cl = lax.broadcasted_iota(jnp.int32, (LANE, LANE), 1)
    same = (rj // CHUNK) == (cl // CHUNK)
    m_incl_t = (jnp.logical_and(same, rj % CHUNK <= cl % CHUNK).astype(F32),
                jnp.logical_and(same, rj % CHUNK >= cl % CHUNK).astype(F32))
    ones_bd = same.astype(F32)

    chains = []
    for j in range(CPT):
        rows = slice(j * CHUNK, (j + 1) * CHUNK)
        bg = bg_ref[rows, :]
        bgt = bgt_ref[j]
        gam = (_hdot(m_incl[0], bg), _hdot(m_incl[1], bg))
        gam_t = (_hdot(bgt, m_incl_t[0]), _hdot(bgt, m_incl_t[1]))
        gt_rows = _hdot(bgt, ones_bd)
        gcx_ref[j * 8:(j + 1) * 8, :] = jnp.exp(gt_rows[8:16])
        for p in range(N_PAIRS):
            ls = slice(p * LANE, (p + 1) * LANE)
            q2 = qkv_ref[rows, p * LANE:(p + 1) * LANE]
            k2 = qkv_ref[rows, qk + p * LANE:qk + (p + 1) * LANE]
            v2 = qkv_ref[rows, 2 * qk + p * LANE:2 * qk + (p + 1) * LANE]
            left = _left_half(k2.shape)
            zero = jnp.zeros_like(k2)
            k_bd = jnp.concatenate([jnp.where(left, k2, zero), jnp.where(left, zero, k2)], axis=0)
            kq = lax.dot_general(jnp.concatenate([k2, q2], axis=0), k_bd, (((1,), (1,)), ((), ())),
                                 preferred_element_type=F32)
            kk2, qk2 = kq[:CHUNK], kq[CHUNK:]
            k2f, q2f, v2f = k2.astype(F32), q2.astype(F32), v2.astype(F32)
            for d in range(2):
                c0 = d * GDN_HEADS + 2 * p
                r = 8 + d * N_PAIRS + p
                b2 = _lane_bcast_pair(bg, c0)
                gcol2 = _lane_bcast_pair(gam[d], nh2 + c0)
                grow2 = gam_t[d][r:r + 1, :]
                gt2 = gt_rows[r:r + 1, :]
                decay2 = jnp.where(incl[d], jnp.exp(jnp.where(incl[d], gcol2 - grow2, 0.0)), 0.0)
                n2 = -jnp.where(strict[d], b2 * decay2 * kk2, 0.0)
                eg2 = jnp.exp(gcol2)
                rhs = jnp.concatenate([(b2 * eg2) * k2f, b2 * v2f], axis=1)
                kd_ref[d, rows, ls] = (k2f * jnp.exp(gt2 - gcol2)).astype(BF16)
                qg_ref[d, rows, ls] = (q2f * eg2).astype(BF16)
                pm_ref[d, rows, ls] = (decay2 * qk2).astype(BF16)
                chains.append(dict(d=d, rows=rows, ls=ls, n=n2, pw=n2, t=eye2 + n2, rhs=rhs))

    for ch in chains:
        ch["pw"] = jnp.dot(ch["pw"].astype(BF16), _blockdiag(ch["pw"]), preferred_element_type=F32)
    for _ in range(4):
        for ch in chains:
            res = jnp.dot(jnp.concatenate([ch["pw"], ch["t"]], axis=0).astype(BF16), _blockdiag(ch["pw"]),
                          preferred_element_type=F32)
            ch["pw"] = res[:CHUNK]
            ch["t"] = ch["t"] + res[CHUNK:]
    for ch in chains:
        ch["t"] = ch["t"] + jnp.dot(ch["t"].astype(BF16), _blockdiag(ch["pw"]), preferred_element_type=F32)
    for ch in chains:
        ch["tb"] = ch["t"].astype(BF16)
        ch["x"] = jnp.dot(ch["tb"], _blockdiag_pair(ch["rhs"]), preferred_element_type=F32)
    for ch in chains:
        n_hi, n_lo = _hi_lo(ch["n"])
        x_hi, x_lo = _hi_lo(ch["x"])
        top = jnp.dot(jnp.concatenate([n_hi, n_lo], axis=0), _blockdiag_pair(x_hi), preferred_element_type=F32)
        nx = top[:CHUNK] + top[CHUNK:] + jnp.dot(n_hi, _blockdiag_pair(x_lo), preferred_element_type=F32)
        ch["r"] = ch["rhs"] - ch["x"] + nx
    for ch in chains:
        x = ch["x"] + jnp.dot(ch["tb"], _blockdiag_pair(ch["r"]), preferred_element_type=F32)
        w_ref[ch["d"], ch["rows"], ch["ls"]] = x[:, :LANE].astype(BF16)
        u0_ref[ch["d"], ch["rows"], ch["ls"]] = x[:, LANE:]


def _gdn_local_call(qkvn, bg, bgt2):
    m, c = qkvn.shape
    nt = m // TM
    dv = GDN_HEADS * GDN_DV
    spec2 = pl.BlockSpec((2, TM, dv), lambda t: (0, t, 0))
    shp = lambda dt: jax.ShapeDtypeStruct((2, m, dv), dt)
    return pl.pallas_call(
        _gdn_local_kernel,
        grid=(nt,),
        in_specs=[pl.BlockSpec((TM, c), lambda t: (t, 0)), pl.BlockSpec((TM, LANE), lambda t: (t, 0)),
                  pl.BlockSpec((CPT, 16, LANE), lambda t: (t, 0, 0))],
        out_specs=[spec2, spec2, spec2, spec2, spec2, pl.BlockSpec((CPT * 8, LANE), lambda t: (t, 0))],
        out_shape=[shp(BF16), shp(F32), shp(BF16), shp(BF16), shp(BF16),
                   jax.ShapeDtypeStruct((m // CHUNK * 8, LANE), F32)],
        compiler_params=_cparams(),
        name="gdn_local",
    )(qkvn, bg, bgt2)


def _gdn_scan_kernel(wf, u0f, kdf, qgf, pmf, gcf, wb, u0b, kdb, qgb, pmb, gcb, of_ref, ob_ref, s_ref):
    @pl.when(pl.program_id(1) == 0)
    def _():
        s_ref[...] = jnp.zeros_like(s_ref)

    refs = ((wf, u0f, kdf, qgf, pmf, gcf, of_ref), (wb, u0b, kdb, qgb, pmb, gcb, ob_ref))
    left = _left_half((CHUNK, LANE))
    state = [s_ref[i] for i in range(2 * N_PAIRS)]
    for sub in range(CPT):
        cur = []
        for d in range(2):
            j = sub if d == 0 else CPT - 1 - sub
            rows = slice(j * CHUNK, (j + 1) * CHUNK)
            for p in range(N_PAIRS):
                cur.append((d, p, rows, slice(p * LANE, (p + 1) * LANE), j * 8 + d * N_PAIRS + p))
        res = []
        for d, p, rows, ls, gr in cur:
            w_r, _, _, qg_r, _, _, _ = refs[d]
            lhs = jnp.concatenate([w_r[rows, ls], qg_r[rows, ls]], axis=0)
            res.append(jnp.dot(lhs, _blockdiag(state[d * N_PAIRS + p]), preferred_element_type=F32))
        us = []
        for (d, p, rows, ls, gr), r in zip(cur, res):
            us.append(refs[d][1][rows, ls] - r[:CHUNK])
        for (d, p, rows, ls, gr), r, u in zip(cur, res, us):
            _, _, kd_r, _, pm_r, gc_r, o_r = refs[d]
            o_r[rows, ls] = r[CHUNK:] + jnp.dot(pm_r[rows, ls], _blockdiag(u), preferred_element_type=F32)
            upd = lax.dot_general(kd_r[rows, ls], u.astype(BF16), (((0,), (0,)), ((), ())),
                                  preferred_element_type=F32)
            i = d * N_PAIRS + p
            state[i] = gc_r[gr:gr + 1, :] * state[i] + jnp.where(left, upd[:CHUNK], upd[CHUNK:])
    for i in range(2 * N_PAIRS):
        s_ref[i] = state[i]


def _gdn_scan_call(w, u0, kd, qg, pm, gcx, n_batch, tpb, n_ctx_tiles):
    m, dv = w.shape[1], w.shape[2]

    def fwd_tile(b, j):
        return b * tpb + j

    def bwd_tile(b, j):
        return b * tpb + jnp.where(j < n_ctx_tiles, n_ctx_tiles - 1 - j, tpb - 1 + n_ctx_tiles - j)

    def specs(d, tile):
        big = pl.BlockSpec((None, TM, dv), lambda b, j: (d, tile(b, j), 0))
        return [big, big, big, big, big, pl.BlockSpec((CPT * 8, LANE), lambda b, j: (tile(b, j), 0))]

    return pl.pallas_call(
        _gdn_scan_kernel,
        grid=(n_batch, tpb),
        in_specs=specs(0, fwd_tile) + specs(1, bwd_tile),
        out_specs=[pl.BlockSpec((TM, dv), lambda b, j: (fwd_tile(b, j), 0)),
                   pl.BlockSpec((TM, dv), lambda b, j: (bwd_tile(b, j), 0))],
        out_shape=[jax.ShapeDtypeStruct((m, dv), F32)] * 2,
        scratch_shapes=[pltpu.VMEM((2 * N_PAIRS, GDN_DK, LANE), F32)],
        compiler_params=_cparams(2),
        name="gdn_scan",
    )(w, u0, kd, qg, pm, gcx, w, u0, kd, qg, pm, gcx)


def _merge_kernel(za_ref, zb_ref, of_ref, ob_ref, gg_ref, mg_ref, x_ref, mod_ref, wa_ref, wb_ref, wc_ref, wo_ref,
                  ones_ref, par_ref, go_ref, o_ref, *, alpha):
    oc = of_ref[...] + ob_ref[...]
    ms = _group_sumsq(oc, ones_ref[...]) * (1.0 / GDN_DV)
    zc = oc * lax.rsqrt(ms + EPS) * go_ref[...] * _silu(gg_ref[...])
    ya = jnp.dot(za_ref[...], wa_ref[...], preferred_element_type=F32)
    yb = jnp.dot(zb_ref[...], wb_ref[...], preferred_element_type=F32)
    yc = _bdot(zc, wc_ref[...])
    d = D_MODEL
    mix = (mg_ref[:, 0:d].astype(F32) * ya + mg_ref[:, d:2 * d].astype(F32) * yb
           + mg_ref[:, 2 * d:3 * d].astype(F32) * yc)
    y = _bdot(mix, wo_ref[...])
    g1 = mod_ref[0, 2:3, :]
    o_ref[...] = _layernorm(alpha * x_ref[...] + g1 * y, par_ref[0:1, :], par_ref[1:2, :])


def _tile_maps(n_batch, tpb, n_ctx_tiles, lat_only):
    if lat_only:
        lpb = tpb - n_ctx_tiles
        return n_batch * lpb, (lambda t: (t // lpb) * tpb + n_ctx_tiles + t % lpb), (lambda t: t // lpb)
    return n_batch * tpb, (lambda t: t), (lambda t: jnp.where(t % tpb < n_ctx_tiles, n_batch, t // tpb))


def _merge_call(za, zb, of, ob, gg, mg, xcat, modrows, wa, wb, wc, wo, ones_blk, par, go, alpha, tpb, n_ctx_tiles,
                lat_only):
    d = xcat.shape[1]
    nt, src_tile, mod_row = _tile_maps(modrows.shape[0] - 1, tpb, n_ctx_tiles, lat_only)
    row = lambda w: pl.BlockSpec((TM, w), lambda t: (src_tile(t), 0))
    own = lambda w: pl.BlockSpec((TM, w), lambda t: (t, 0))
    return pl.pallas_call(
        functools.partial(_merge_kernel, alpha=alpha),
        grid=(nt,),
        in_specs=[row(za.shape[1]), own(zb.shape[1]), row(of.shape[1]), row(ob.shape[1]), row(gg.shape[1]),
                  row(mg.shape[1]), row(d),
                  pl.BlockSpec((1, 6, d), lambda t: (mod_row(t), 0, 0)),
                  _const_spec(wa.shape), _const_spec(wb.shape), _const_spec(wc.shape), _const_spec(wo.shape),
                  _const_spec(ones_blk.shape), _const_spec(par.shape), _const_spec(go.shape)],
        out_specs=own(d),
        out_shape=jax.ShapeDtypeStruct((nt * TM, d), F32),
        compiler_params=_cparams(),
        name="merge",
    )(za, zb, of, ob, gg, mg, xcat, modrows, wa, wb, wc, wo, ones_blk, par, go)


ROW_TILES = D_MODEL // LANE


def _to_token_tiles(x):
    chunks = jnp.stack([x[:, s * LANE:(s + 1) * LANE] for s in range(ROW_TILES)], axis=0)
    return pltpu.einshape("stl->tsl", chunks)


def _from_token_tiles(x3):
    chunks = pltpu.einshape("tsl->stl", x3)
    return [chunks[s] for s in range(ROW_TILES)]


def _router_kernel(x_ref, mod_ref, w_ref, b_ref, h_ref, r_ref, cnt_ref, cnt_scr):
    @pl.when(pl.program_id(0) == 0)
    def _():
        cnt_scr[...] = jnp.zeros_like(cnt_scr)

    sh = mod_ref[0, 3:4, :]
    sc = mod_ref[0, 4:5, :]
    h = x_ref[...] * (1.0 + sc) + sh
    h_ref[...] = _to_token_tiles(h)
    h_hi, h_lo = _hi_lo(h)
    lg = (jnp.dot(h_hi, w_ref[0], preferred_element_type=F32) + jnp.dot(h_lo, w_ref[0], preferred_element_type=F32)
          + jnp.dot(h_hi, w_ref[1], preferred_element_type=F32)) + b_ref[...]
    lane = lax.broadcasted_iota(jnp.int32, lg.shape, 1)
    neg = jnp.float32(-1e30)
    big = jnp.int32(1 << 20)
    is_g = lane < N_GROUPS
    gmax = jnp.max(jnp.where(is_g, lg, neg), axis=-1, keepdims=True)
    gsum = jnp.sum(jnp.where(is_g, jnp.exp(lg - gmax), 0.0), axis=-1, keepdims=True)
    gp = 1.0 / gsum
    gi = jnp.min(jnp.where(jnp.logical_and(is_g, lg == gmax), lane, big), axis=-1, keepdims=True)
    lo = N_GROUPS + gi * EXP_PER_GROUP
    in_grp = jnp.logical_and(lane >= lo, lane < lo + EXP_PER_GROUP)
    v1 = jnp.max(jnp.where(in_grp, lg, neg), axis=-1, keepdims=True)
    l1 = jnp.min(jnp.where(jnp.logical_and(in_grp, lg == v1), lane, big), axis=-1, keepdims=True)
    rest = jnp.logical_and(in_grp, lane != l1)
    v2 = jnp.max(jnp.where(rest, lg, neg), axis=-1, keepdims=True)
    l2 = jnp.min(jnp.where(jnp.logical_and(rest, lg == v2), lane, big), axis=-1, keepdims=True)
    e2 = jnp.exp(v2 - v1)
    p1 = 1.0 / (1.0 + e2)
    p2 = e2 / (1.0 + e2)
    out = jnp.where(lane == 0, gp * p1, 0.0)
    out = jnp.where(lane == 1, gp * p2, out)
    out = jnp.where(lane == 2, (l1 - N_GROUPS).astype(F32), out)
    out = jnp.where(lane == 3, (l2 - N_GROUPS).astype(F32), out)
    chosen = jnp.logical_or(lane == l1, lane == l2)
    onehot = jnp.where(chosen, 1.0, 0.0)
    ri = lax.broadcasted_iota(jnp.int32, (TM, TM), 0)
    ci = lax.broadcasted_iota(jnp.int32, (TM, TM), 1)
    earlier = jnp.dot(jnp.where(ri > ci, 1.0, 0.0).astype(BF16), onehot.astype(BF16), preferred_element_type=F32)
    before = earlier + cnt_scr[0:1, :]
    out = jnp.where(lane == 4, jnp.sum(jnp.where(lane == l1, before, 0.0), axis=-1, keepdims=True), out)
    out = jnp.where(lane == 5, jnp.sum(jnp.where(lane == l2, before, 0.0), axis=-1, keepdims=True), out)
    r_ref[...] = out
    cnt = cnt_scr[...] + jnp.sum(onehot, axis=0, keepdims=True)
    cnt_scr[...] = cnt
    cnt_ref[...] = cnt


def _router_call(x1, modrows, wr, br, mod_row):
    m, d = x1.shape
    nt = m // TM
    return pl.pallas_call(
        _router_kernel,
        grid=(nt,),
        in_specs=[pl.BlockSpec((TM, d), lambda t: (t, 0)), pl.BlockSpec((1, 6, d), lambda t: (mod_row(t), 0, 0)),
                  _const_spec(wr.shape), _const_spec(br.shape)],
        out_specs=[pl.BlockSpec((TM, ROW_TILES, LANE), lambda t: (t, 0, 0)), pl.BlockSpec((TM, LANE), lambda t: (t, 0)),
                   pl.BlockSpec((8, LANE), lambda t: (0, 0))],
        out_shape=[jax.ShapeDtypeStruct((m, ROW_TILES, LANE), F32), jax.ShapeDtypeStruct((m, LANE), F32),
                   jax.ShapeDtypeStruct((8, LANE), F32)],
        scratch_shapes=[pltpu.VMEM((8, LANE), F32)],
        compiler_params=_cparams(),
        name="router",
    )(x1, modrows, wr, br)


GATHER_UNROLL = 16


def _row_copy(src_hbm, dst, sem, row, j):
    return pltpu.make_async_copy(src_hbm.at[pl.ds(row, 1)], dst.at[pl.ds(j, 1)], sem)


def _start_gather(src_hbm, idx_ref, buf, sem, slot, n_rows):
    def body(jj, carry):
        for u in range(GATHER_UNROLL):
            j = jj * GATHER_UNROLL + u
            _row_copy(src_hbm, buf.at[slot], sem.at[slot], idx_ref[0, 0, j], j).start(priority=u % 2)
        return carry
    lax.fori_loop(0, n_rows // GATHER_UNROLL, body, 0)


def _wait_gather(src_hbm, buf, sem, slot, n_rows):
    def body(j, carry):
        _row_copy(src_hbm, buf.at[slot], sem.at[slot], 0, j).wait()
        return carry
    lax.fori_loop(0, n_rows, body, 0, unroll=8)


def _pipelined_gather(src_hbm, idx_ref, idx_next_ref, buf, sem, n_rows):
    i = pl.program_id(0)
    n = pl.num_programs(0)
    slot = i % 2

    @pl.when(i == 0)
    def _():
        _start_gather(src_hbm, idx_ref, buf, sem, 0, n_rows)

    @pl.when(i + 1 < n)
    def _():
        _start_gather(src_hbm, idx_next_ref, buf, sem, 1 - slot, n_rows)

    _wait_gather(src_hbm, buf, sem, slot, n_rows)
    return slot


def _expert_kernel(blk_exp_ref, tok_ref, tok_next_ref, h_hbm, w1_ref, w3_ref, w2_ref, o_ref, buf, sem):
    slot = _pipelined_gather(h_hbm, tok_ref, tok_next_ref, buf, sem, MOE_ROWS)
    x = jnp.concatenate(_from_token_tiles(buf[slot]), axis=-1).astype(BF16)
    act = (_silu(_bdot(x, w1_ref[0])) * _bdot(x, w3_ref[0])).astype(BF16)
    o_ref[...] = _to_token_tiles(_bdot(act, w2_ref[0]))


def _expert_call(blk_exp, tok_buf, h3, w1, w3, w2, l):
    n_blk = blk_exp.shape[0]
    d = D_MODEL
    tok3 = tok_buf.reshape(n_blk, 1, MOE_ROWS)
    grid_spec = pltpu.PrefetchScalarGridSpec(
        num_scalar_prefetch=1,
        grid=(n_blk,),
        in_specs=[pl.BlockSpec((1, 1, MOE_ROWS), lambda i, be: (i, 0, 0), memory_space=pltpu.SMEM),
                  pl.BlockSpec((1, 1, MOE_ROWS), lambda i, be: (jnp.minimum(i + 1, n_blk - 1), 0, 0),
                               memory_space=pltpu.SMEM),
                  pl.BlockSpec(memory_space=pl.ANY),
                  pl.BlockSpec((None, 1, d, D_EXPERT), lambda i, be: (l, be[i], 0, 0)),
                  pl.BlockSpec((None, 1, d, D_EXPERT), lambda i, be: (l, be[i], 0, 0)),
                  pl.BlockSpec((None, 1, D_EXPERT, d), lambda i, be: (l, be[i], 0, 0))],
        out_specs=pl.BlockSpec((MOE_ROWS, ROW_TILES, LANE), lambda i, be: (i, 0, 0)),
        scratch_shapes=[pltpu.VMEM((2, MOE_ROWS, ROW_TILES, LANE), F32), pltpu.SemaphoreType.DMA((2,))],
    )
    return pl.pallas_call(
        _expert_kernel,
        grid_spec=grid_spec,
        out_shape=jax.ShapeDtypeStruct((n_blk * MOE_ROWS, ROW_TILES, LANE), F32),
        compiler_params=_cparams(),
        name="expert_ffn",
    )(blk_exp, tok3, tok3, h3, w1, w3, w2)


def _combine_kernel(pos_ref, pos_next_ref, y_hbm, x_ref, r_ref, mod_ref, par_ref, o_ref, buf, sem, *, alpha):
    slot = _pipelined_gather(y_hbm, pos_ref, pos_next_ref, buf, sem, 2 * TM)
    r = r_ref[...]
    g0 = jnp.broadcast_to(r[:, 0:1], (TM, LANE))
    g1 = jnp.broadcast_to(r[:, 1:2], (TM, LANE))
    y2 = jnp.concatenate([g0 * c[:TM] + g1 * c[TM:] for c in _from_token_tiles(buf[slot])], axis=-1)
    g2 = mod_ref[0, 5:6, :]
    o_ref[...] = _layernorm(alpha * x_ref[...] + g2 * y2, par_ref[0:1, :], par_ref[1:2, :])


def _combine_call(pos, yb, x1, route, modrows, par, alpha, mod_row):
    m, d = x1.shape
    nt = m // TM
    pos3 = pos.reshape(nt, 1, 2 * TM)

    def mod_map(t):
        return (mod_row(t), 0, 0)

    return pl.pallas_call(
        functools.partial(_combine_kernel, alpha=alpha),
        grid=(nt,),
        in_specs=[pl.BlockSpec((1, 1, 2 * TM), lambda t: (t, 0, 0), memory_space=pltpu.SMEM),
                  pl.BlockSpec((1, 1, 2 * TM), lambda t: (jnp.minimum(t + 1, nt - 1), 0, 0), memory_space=pltpu.SMEM),
                  pl.BlockSpec(memory_space=pl.ANY),
                  pl.BlockSpec((TM, d), lambda t: (t, 0)), pl.BlockSpec((TM, LANE), lambda t: (t, 0)),
                  pl.BlockSpec((1, 6, d), mod_map), _const_spec(par.shape)],
        out_specs=pl.BlockSpec((TM, d), lambda t: (t, 0)),
        out_shape=jax.ShapeDtypeStruct((m, d), F32),
        scratch_shapes=[pltpu.VMEM((2, 2 * TM, ROW_TILES, LANE), F32), pltpu.SemaphoreType.DMA((2,))],
        compiler_params=_cparams(),
        name="moe_combine",
    )(pos3, pos3, yb, x1, route, modrows, par)


def _rope_rot_perm():
    nf = QK_ROPE // 4
    src = np.zeros((QK_ROPE,), np.int32)
    sign = np.zeros((QK_ROPE,), np.float32)
    for a in range(2):
        for f in range(nf):
            i0 = a * 2 * nf + f
            i1 = a * 2 * nf + nf + f
            src[i0], sign[i0] = i1, -1.0
            src[i1], sign[i1] = i0, 1.0
    return src, sign


def _rope_tables(n_lat, n_ctx):
    nf = QK_ROPE // 4
    rows = n_lat // GRID_W
    inv = ROPE_BASE ** (-jnp.arange(nf, dtype=F32) / nf)
    r = jnp.repeat(jnp.arange(rows, dtype=F32), GRID_W)
    col = jnp.tile(jnp.arange(GRID_W, dtype=F32), rows)
    ang = jnp.stack([r[:, None] * inv, col[:, None] * inv], axis=1)
    cos = jnp.broadcast_to(jnp.cos(ang)[:, :, None, :], (n_lat, 2, 2, nf)).reshape(n_lat, QK_ROPE)
    sin = jnp.broadcast_to(jnp.sin(ang)[:, :, None, :], (n_lat, 2, 2, nf)).reshape(n_lat, QK_ROPE)
    cos = jnp.concatenate([jnp.ones((n_ctx, QK_ROPE), F32), cos], 0)
    sin = jnp.concatenate([jnp.zeros((n_ctx, QK_ROPE), F32), sin], 0)
    s = n_lat + n_ctx
    z = lambda w: jnp.zeros((s, w), F32)
    pad = LANE - QK_NOPE - QK_ROPE
    q_scale = MLA_SCALE * LOG2_E
    tqc = jnp.concatenate([jnp.full((s, QK_NOPE), q_scale, F32), cos * q_scale, z(pad)], 1)
    tqs = jnp.concatenate([z(QK_NOPE), sin * q_scale, z(pad)], 1)
    tkc = jnp.concatenate([z(QK_NOPE), cos, z(pad)], 1)
    tks = jnp.concatenate([z(QK_NOPE), sin, z(pad)], 1)
    return tqc, tqs, tkc, tks


def _head_pad_cols(w, width):
    k = w.shape[0]
    w = w.reshape(k, MLA_HEADS, width)
    return jnp.pad(w, ((0, 0), (0, 0), (0, HEAD_PAD - width))).reshape(k, MLA_HEADS * HEAD_PAD)


def _layer_weights(l, w_in, b_in, conv_a_w, conv_a_b, ln_a_g, ln_a_b, w_a_out, g_q, g_kv, w_uq, w_qr, w_uk, w_uv,
                   w_b_out, conv_c_w, a_log, dt_bias, g_o, w_c_out, w_out, ln1_g, ln1_b, w_rg, b_rg, w_re, b_re,
                   w1, w3, w2, ln2_g, ln2_b):
    src, sign = _rope_rot_perm()
    wi, bi = w_in[l], b_in[l][None, :]
    sizes = (2 * CONV_CH, Q_RANK, KV_RANK, QK_ROPE, 512, 512, 512, 512, 16, 16, 3 * D_MODEL)
    offs = np.concatenate([[0], np.cumsum(sizes)])

    def cols(a, i, j=None):
        return a[:, offs[i]:offs[(i if j is None else j) + 1]]

    def regroup(a):
        kr = cols(a, 3)
        z = lambda w: jnp.zeros((a.shape[0], w), a.dtype)
        kr_grp = jnp.concatenate([z(QK_NOPE), kr, z(32), z(QK_NOPE), kr[:, src] * sign, z(32)], 1)
        ba_grp = jnp.concatenate([cols(a, 8, 9), z(LANE - 32)], 1)
        return jnp.concatenate([cols(a, 0), cols(a, 1), cols(a, 2), kr_grp, cols(a, 4, 6), cols(a, 7), ba_grp,
                                cols(a, 10)], 1)

    w_all = regroup(wi).astype(BF16)
    b_all = regroup(bi)
    zl = jnp.zeros((LANE,), F32)
    gpar = jnp.stack([zl.at[16:32].set(a_log[l].reshape(-1)), zl.at[16:32].set(dt_bias[l].reshape(-1))], 0)
    gpar = jnp.concatenate([gpar, jnp.zeros((6, LANE), F32)], 0)

    wq = jnp.concatenate([w_uq[l].reshape(Q_RANK, MLA_HEADS, QK_NOPE), w_qr[l].reshape(Q_RANK, MLA_HEADS, QK_ROPE)], 2)
    wq = _head_pad_cols(wq.reshape(Q_RANK, -1), QK_NOPE + QK_ROPE).astype(BF16)
    wqr_rot = (w_qr[l].reshape(Q_RANK, MLA_HEADS, QK_ROPE)[:, :, src] * sign)
    wqr = jnp.concatenate([jnp.zeros((Q_RANK, MLA_HEADS, QK_NOPE), F32), wqr_rot], 2)
    wqr = _head_pad_cols(wqr.reshape(Q_RANK, -1), QK_NOPE + QK_ROPE).astype(BF16)
    wk = _head_pad_cols(w_uk[l], QK_NOPE).astype(BF16)
    wv = _head_pad_cols(w_uv[l], V_DIM).astype(BF16)
    vone = jnp.zeros((MLA_HEADS, HEAD_PAD), F32).at[:, V_DIM].set(1.0).reshape(1, -1)
    wb = jnp.pad(w_b_out[l].reshape(MLA_HEADS, V_DIM, D_MODEL), ((0, 0), (0, HEAD_PAD - V_DIM), (0, 0)))
    wb = wb.reshape(MLA_HEADS * HEAD_PAD, D_MODEL).astype(BF16)

    wr = jnp.concatenate([w_rg[l], w_re[l], jnp.zeros((D_MODEL, LANE - N_GROUPS - N_EXPERTS), F32)], 1)
    wr_hi = wr.astype(BF16)
    wr = jnp.stack([wr_hi, (wr - wr_hi.astype(F32)).astype(BF16)], 0)
    br = jnp.concatenate([b_rg[l], b_re[l], jnp.zeros((LANE - N_GROUPS - N_EXPERTS,), F32)])[None, :]
    return dict(
        w_all=w_all, b_all=b_all, gpar=gpar,
        conv_a_w=conv_a_w[l], conv_a_par=jnp.stack([conv_a_b[l], ln_a_g[l], ln_a_b[l]] + [jnp.zeros_like(ln_a_b[l])] * 5, 0),
        gq=g_q[l][None, :], gkv=g_kv[l][None, :], wq=wq, wqr=wqr, wk=wk, wv=wv, vone=vone,
        conv_c_w=conv_c_w[l],
        wa=w_a_out[l].astype(BF16), wb=wb, wc=w_c_out[l].astype(BF16), wo=w_out[l].astype(BF16),
        go=jnp.tile(g_o[l], GDN_HEADS)[None, :],
        ln1=jnp.stack([ln1_g[l], ln1_b[l]] + [jnp.zeros_like(ln1_g[l])] * 6, 0),
        ln2=jnp.stack([ln2_g[l], ln2_b[l]] + [jnp.zeros_like(ln2_g[l])] * 6, 0),
        wr=wr, br=br,
    )


def _routing_tables(route, cnt, n_tok):
    eid = route[:, 2:4].astype(jnp.int32)
    rank = route[:, 4:6].astype(jnp.int32)
    n_blk = 2 * n_tok // MOE_ROWS + N_EXPERTS
    sizes = cnt[0, N_GROUPS:N_GROUPS + N_EXPERTS].astype(jnp.int32)
    psizes = (sizes + MOE_ROWS - 1) // MOE_ROWS * MOE_ROWS
    pends = jnp.cumsum(psizes)
    pstarts = pends - psizes
    blk_start = jnp.arange(n_blk, dtype=jnp.int32) * MOE_ROWS
    blk_exp = jnp.minimum(jnp.sum((blk_start[:, None] >= pends[None, :]).astype(jnp.int32), axis=1), N_EXPERTS - 1)
    onehot = (eid[:, :, None] == jnp.arange(N_EXPERTS, dtype=jnp.int32)).astype(jnp.int32)
    pos = jnp.sum(onehot * pstarts, axis=-1) + rank
    tok = jnp.broadcast_to(jnp.arange(n_tok, dtype=jnp.int32)[:, None], (n_tok, 2))
    tok_buf = jnp.zeros((n_blk * MOE_ROWS,), jnp.int32).at[pos.reshape(-1)].add(tok.reshape(-1))
    return blk_exp, tok_buf, pos


def kernel(x, c, ctx, c_ctx, w_mod, b_mod, w_in, b_in, conv_a_w, conv_a_b, ln_a_g, ln_a_b, w_a_out, g_q, g_kv, w_uq,
           w_qr, w_uk, w_uv, w_b_out, conv_c_w, a_log, dt_bias, g_o, w_c_out, w_out, ln1_g, ln1_b, w_rg, b_rg, w_re,
           b_re, w1, w3, w2, ln2_g, ln2_b):
    n_batch, n_lat, d = x.shape
    n_ctx = ctx.shape[1]
    depth = w_mod.shape[0]
    alpha = (2 * depth) ** 0.25
    s_len = n_ctx + n_lat
    m = n_batch * s_len
    tpb = s_len // TM
    n_ctx_tiles = n_ctx // TM
    assert n_ctx % TM == 0 and n_lat % TM == 0 and n_ctx % CHUNK == 0 and s_len % n_ctx == 0

    xcat = jnp.concatenate([ctx, x], axis=1).reshape(m, d)
    cvec = jnp.concatenate([c, c_ctx[None, :], jnp.zeros((16 - n_batch - 1, d), F32)], 0)
    tabs = _rope_tables(n_lat, n_ctx)
    ones_blk = jnp.kron(jnp.eye(GDN_HEADS, dtype=F32), jnp.ones((GDN_DK, GDN_DK), F32)).astype(BF16)
    params = (w_in, b_in, conv_a_w, conv_a_b, ln_a_g, ln_a_b, w_a_out, g_q, g_kv, w_uq, w_qr, w_uk, w_uv, w_b_out,
              conv_c_w, a_log, dt_bias, g_o, w_c_out, w_out, ln1_g, ln1_b, w_rg, b_rg, w_re, b_re, w1, w3, w2,
              ln2_g, ln2_b)
    tq = 512 if n_lat % 512 == 0 else TM

    for l in range(depth):
        p = _layer_weights(l, *params)
        mod = _mod_call(cvec, w_mod[l].astype(BF16), b_mod[l][None, :])
        modrows = mod[:n_batch + 1].reshape(n_batch + 1, 6, d)

        u, qd, kvd, kr, gqkv, gg, ba, mg = _inproj_call(xcat, modrows, p["w_all"], p["b_all"], p["gpar"], tpb,
                                                        n_ctx_tiles)
        za = _conv_a_call(u, p["conv_a_w"], p["conv_a_par"], tpb, n_ctx_tiles)
        qp, kp, vp = _mla_proj_call(qd, kvd, kr, tabs, p["gq"], p["gkv"], p["wq"], p["wqr"], p["wk"], p["wv"],
                                    p["vone"], tpb, n_ctx_tiles)
        last = l == depth - 1
        zb = _attn_call(qp, kp, vp, n_batch, s_len, 0, n_lat, s_len, tq)
        if not last:
            zb_c = _attn_call(qp, kp, vp, n_batch, s_len, n_batch * n_lat, n_ctx, n_ctx, n_ctx)
            hp = MLA_HEADS * HEAD_PAD
            zb = jnp.concatenate([zb_c.reshape(n_batch, n_ctx, hp), zb.reshape(n_batch, n_lat, hp)], 1).reshape(m, hp)
        qkvn = _gdn_prep_call(gqkv, p["conv_c_w"], ones_blk, tpb, n_ctx_tiles)
        bgt2 = ba[:, :32].reshape(m // CHUNK, CHUNK, 2, 2, N_PAIRS, 2).transpose(0, 2, 3, 4, 5, 1)
        bgt2 = bgt2.reshape(m // CHUNK, 16, LANE)
        gw, gu0, gkd, gqg, gpm, gcx = _gdn_local_call(qkvn, ba, bgt2)
        of, ob = _gdn_scan_call(gw, gu0, gkd, gqg, gpm, gcx, n_batch, tpb, n_ctx_tiles)
        x1 = _merge_call(za, zb, of, ob, gg, mg, xcat, modrows, p["wa"], p["wb"], p["wc"], p["wo"], ones_blk, p["ln1"],
                         p["go"], alpha, tpb, n_ctx_tiles, last)
        n_tok = x1.shape[0]
        _, _, mod_row = _tile_maps(n_batch, tpb, n_ctx_tiles, last)
        h3, route, cnt = _router_call(x1, modrows, p["wr"], p["br"], mod_row)
        blk_exp, tok_buf, pos = _routing_tables(route, cnt, n_tok)
        yb = _expert_call(blk_exp, tok_buf, h3, w1, w3, w2, l)
        pos_t = pos.reshape(n_tok // TM, TM, 2).transpose(0, 2, 1).reshape(-1)
        xcat = _combine_call(pos_t, yb, x1, route, modrows, p["ln2"], alpha, mod_row)

    return xcat.reshape(n_batch, n_lat, d)
```
